```python
import jax, jax.numpy as jnp
from jax import lax
import numpy as np

D_MODEL = 1024
BATCH = 16
SEQ = 2048
DEPTH = 2

HEAD_DIM = 64
ROPE_THETA = 10000.0
EPS = 1e-6
SWA_Q_HEADS = 8
SWA_KV_HEADS = 2
SWA_WINDOW = 128
SWA_BLOCK = 128
MOBA_HEADS = 8
MOBA_BLOCK = 256
MOBA_TOPK = 3
MOBA_Q_CHUNK = 16
GLA_HEADS = 4
GLA_DK = 64
GLA_DV = 128
GLA_GATE_RANK = 16
GLA_TAU = 16.0
GLA_CHUNK = 64
N_BRANCH = 3
SWA_WIDTH = SWA_Q_HEADS * HEAD_DIM
MOBA_WIDTH = MOBA_HEADS * HEAD_DIM
GLA_WIDTH = GLA_HEADS * GLA_DV
D_FF = -(-8 * D_MODEL // (3 * 256)) * 256
IN_SIZES = (
    SWA_Q_HEADS * HEAD_DIM, SWA_KV_HEADS * HEAD_DIM, SWA_KV_HEADS * HEAD_DIM,
    MOBA_HEADS * HEAD_DIM, MOBA_HEADS * HEAD_DIM, MOBA_HEADS * HEAD_DIM,
    GLA_HEADS * GLA_DK, GLA_HEADS * GLA_DK, GLA_HEADS * GLA_DV,
    GLA_GATE_RANK, GLA_HEADS * GLA_DV,
    N_BRANCH * D_MODEL,
)
D_IN = sum(IN_SIZES)

kernel_name = "hybrid_swa_moba_gla_block"


def rms_norm(x, g):
    xf = x.astype(jnp.float32)
    y = xf * lax.rsqrt(jnp.mean(xf * xf, axis=-1, keepdims=True) + EPS)
    return (y * g.astype(jnp.float32)).astype(x.dtype)


def rope_tables(positions, dim, dtype):
    inv_freq = ROPE_THETA ** (-jnp.arange(0, dim, 2, dtype=jnp.float32) / dim)
    ang = positions.astype(jnp.float32)[..., None] * inv_freq
    return jnp.cos(ang)[:, :, None, :].astype(dtype), jnp.sin(ang)[:, :, None, :].astype(dtype)


def apply_rope(x, cos, sin):
    x1, x2 = jnp.split(x, 2, axis=-1)
    return jnp.concatenate([x1 * cos - x2 * sin, x2 * cos + x1 * sin], axis=-1)


def sliding_window_attention(q, k, v, sinks):
    B, S, Hq, d = q.shape
    Hkv = k.shape[2]
    G = Hq // Hkv
    L = SWA_BLOCK
    nb = S // L
    qb = q.reshape(B, nb, L, Hkv, G, d)
    pad = lambda t: jnp.pad(t, ((0, 0), (L, 0), (0, 0), (0, 0))).reshape(B, nb + 1, L, Hkv, d)
    kp, vp = pad(k), pad(v)
    kb = jnp.concatenate([kp[:, :-1], kp[:, 1:]], axis=2)
    vb = jnp.concatenate([vp[:, :-1], vp[:, 1:]], axis=2)
    s = jnp.einsum('bnqhgd,bnkhd->bnhgqk', qb, kb).astype(jnp.float32) * (d ** -0.5)
    qi = jnp.arange(L)[:, None]
    ki = jnp.arange(2 * L)[None, :]
    rel = qi + L - ki
    kpos = jnp.arange(nb)[:, None, None] * L + ki - L
    allowed = (rel >= 0) & (rel < SWA_WINDOW) & (kpos >= 0)
    s = jnp.where(allowed[None, :, None, None], s, -jnp.inf)
    sink = sinks.astype(jnp.float32).reshape(1, 1, Hkv, G, 1, 1)
    m = jnp.maximum(jnp.max(s, axis=-1, keepdims=True), sink)
    p = jnp.exp(s - m)
    p = (p / (jnp.sum(p, axis=-1, keepdims=True) + jnp.exp(sink - m))).astype(v.dtype)
    o = jnp.einsum('bnhgqk,bnkhd->bnqhgd', p, vb)
    return o.reshape(B, S, Hq * d)


def moba_attention(q, k, v):
    B, S, H, d = q.shape
    L = MOBA_BLOCK
    nb = -(-S // L)
    Sp = nb * L
    scale = d ** -0.5
    padS = lambda t: jnp.pad(t, ((0, 0), (0, Sp - S), (0, 0), (0, 0))).transpose(0, 2, 1, 3)
    qh, kh, vh = padS(q), padS(k), padS(v)
    kblk = kh.reshape(B, H, nb, L, d)
    vblk = vh.reshape(B, H, nb, L, d)
    kmean = jnp.mean(kblk.astype(jnp.float32), axis=3)
    gate = jnp.einsum('bhtd,bhnd->bhtn', qh.astype(jnp.float32), kmean)
    past = jnp.arange(nb)[None, :] < (jnp.arange(Sp) // L)[:, None]
    gate = jnp.where(past, gate, -jnp.inf)
    topk = min(MOBA_TOPK, nb)
    gval, gidx = lax.top_k(gate, topk)
    gvalid = jnp.isfinite(gval)

    C = MOBA_Q_CHUNK
    nc = Sp // C
    to_chunks = lambda t: jnp.moveaxis(t.reshape(B, H, nc, C, t.shape[-1]), 2, 0)
    bi = jnp.arange(B)[:, None, None, None]
    hi = jnp.arange(H)[None, :, None, None]

    def one_chunk(args):
        qc, ic, vc_ok, c = args
        ksel = kblk[bi, hi, ic]
        vsel = vblk[bi, hi, ic]
        s_sel = jnp.einsum('bhqd,bhqnkd->bhqnk', qc, ksel).astype(jnp.float32) * scale
        s_sel = jnp.where(vc_ok[..., None], s_sel, -jnp.inf).reshape(B, H, C, topk * L)
        own = (c * C) // L
        kown = lax.dynamic_index_in_dim(kblk, own, axis=2, keepdims=False)
        vown = lax.dynamic_index_in_dim(vblk, own, axis=2, keepdims=False)
        s_own = jnp.einsum('bhqd,bhkd->bhqk', qc, kown).astype(jnp.float32) * scale
        qpos = c * C + jnp.arange(C)
        kpos = own * L + jnp.arange(L)
        s_own = jnp.where(kpos[None, :] <= qpos[:, None], s_own, -jnp.inf)
        p = jax.nn.softmax(jnp.concatenate([s_sel, s_own], axis=-1), axis=-1).astype(vh.dtype)
        p_sel = p[..., :topk * L].reshape(B, H, C, topk, L)
        p_own = p[..., topk * L:]
        return (jnp.einsum('bhqnk,bhqnkd->bhqd', p_sel, vsel)
                + jnp.einsum('bhqk,bhkd->bhqd', p_own, vown))

    out = lax.map(one_chunk, (to_chunks(qh), to_chunks(gidx), to_chunks(gvalid),
                              jnp.arange(nc, dtype=jnp.int32)))
    out = jnp.moveaxis(out, 0, 2).reshape(B, H, Sp, d)[:, :, :S]
    return out.transpose(0, 2, 1, 3).reshape(B, S, H * d)


def gla_attention(q, k, v, log_alpha):
    B, S, H, dk = q.shape
    dv = v.shape[-1]
    C = GLA_CHUNK
    nc = S // C
    f32 = jnp.float32
    rs = lambda t: t.astype(f32).reshape(B, nc, C, H, t.shape[-1]).transpose(0, 3, 1, 2, 4)
    qc = rs(q) * (dk ** -0.5)
    kc, vc = rs(k), rs(v)
    b = jnp.cumsum(rs(log_alpha), axis=3)
    b_last = b[..., -1:, :]
    q_dec = qc * jnp.exp(b)
    k_dec = kc * jnp.exp(-b)
    k_tail = kc * jnp.exp(b_last - b)
    A = jnp.einsum('bhnid,bhnjd->bhnij', q_dec, k_dec)
    A = jnp.where(jnp.tril(jnp.ones((C, C), dtype=bool)), A, 0.0)
    o_intra = jnp.einsum('bhnij,bhnjv->bhniv', A, vc)
    dS = jnp.einsum('bhnjd,bhnjv->bhndv', k_tail, vc)
    decay = jnp.exp(b_last[..., 0, :])

    def step(state, inp):
        dec, ds = inp
        return dec[..., None] * state + ds, state

    _, s_before = lax.scan(step, jnp.zeros((B, H, dk, dv), f32),
                           (jnp.moveaxis(decay, 2, 0), jnp.moveaxis(dS, 2, 0)))
    o_inter = jnp.einsum('bhnid,bhndv->bhniv', q_dec, jnp.moveaxis(s_before, 0, 2))
    o = o_intra + o_inter
    return o.transpose(0, 2, 3, 1, 4).reshape(B, S, H, dv).astype(v.dtype)


def hybrid_mixer(h, cos, sin, w_in, swa_q_norm, swa_k_norm, swa_sinks, moba_q_norm, moba_k_norm,
                 gla_w_alpha, gla_b_alpha, gla_out_norm, w_branch_swa, w_branch_moba, w_branch_gla, w_out):
    B, S, D = h.shape
    proj = h @ w_in
    points = np.cumsum(IN_SIZES)[:-1].tolist()
    (qa, ka, va, qb, kb, vb, qg, kg, vg, a_low, r_gate, gates) = jnp.split(proj, points, axis=-1)
    heads = lambda t, n: t.reshape(B, S, n, -1)
    qa = apply_rope(rms_norm(heads(qa, SWA_Q_HEADS), swa_q_norm), cos, sin)
    ka = apply_rope(rms_norm(heads(ka, SWA_KV_HEADS), swa_k_norm), cos, sin)
    y_a = sliding_window_attention(qa, ka, heads(va, SWA_KV_HEADS), swa_sinks)
    qb = apply_rope(rms_norm(heads(qb, MOBA_HEADS), moba_q_norm), cos, sin)
    kb = apply_rope(rms_norm(heads(kb, MOBA_HEADS), moba_k_norm), cos, sin)
    y_b = moba_attention(qb, kb, heads(vb, MOBA_HEADS))
    log_alpha = jax.nn.log_sigmoid((a_low @ gla_w_alpha + gla_b_alpha).astype(jnp.float32)) / GLA_TAU
    o_c = gla_attention(heads(qg, GLA_HEADS), heads(kg, GLA_HEADS), heads(vg, GLA_HEADS),
                        log_alpha.reshape(B, S, GLA_HEADS, GLA_DK))
    y_c = rms_norm(o_c, gla_out_norm).reshape(B, S, GLA_WIDTH) * jax.nn.silu(r_gate)
    g = jax.nn.sigmoid(gates.reshape(B, S, N_BRANCH, D))
    merged = (g[:, :, 0] * (y_a @ w_branch_swa)
              + g[:, :, 1] * (y_b @ w_branch_moba)
              + g[:, :, 2] * (y_c @ w_branch_gla))
    return merged @ w_out


def swiglu(h, w_gate, w_up, w_down):
    return (jax.nn.silu(h @ w_gate) * (h @ w_up)) @ w_down


def setup_inputs(seed: int = 0) -> dict:
    key = jax.random.key(seed)
    ks = jax.random.split(key, 24)
    f32 = jnp.float32
    nrm = lambda k, shape, scale: jax.random.normal(k, shape, f32) * scale
    gain = lambda k, shape: 1.0 + 0.05 * jax.random.normal(k, shape, f32)
    return {
        "x": nrm(ks[0], (BATCH, SEQ, D_MODEL), 1.0),
        "positions": jnp.broadcast_to(jnp.arange(SEQ, dtype=jnp.int32), (BATCH, SEQ)),
        "attn_norm": gain(ks[1], (DEPTH, D_MODEL)),
        "w_in": nrm(ks[2], (DEPTH, D_MODEL, D_IN), D_MODEL ** -0.5),
        "swa_q_norm": gain(ks[3], (DEPTH, HEAD_DIM)),
        "swa_k_norm": gain(ks[4], (DEPTH, HEAD_DIM)),
        "swa_sinks": nrm(ks[5], (DEPTH, SWA_Q_HEADS), 0.5),
        "moba_q_norm": gain(ks[6], (DEPTH, HEAD_DIM)),
        "moba_k_norm": gain(ks[7], (DEPTH, HEAD_DIM)),
        "gla_w_alpha": nrm(ks[8], (DEPTH, GLA_GATE_RANK, GLA_HEADS * GLA_DK), GLA_GATE_RANK ** -0.5),
        "gla_b_alpha": nrm(ks[9], (DEPTH, GLA_HEADS * GLA_DK), 0.1),
        "gla_out_norm": gain(ks[10], (DEPTH, GLA_DV)),
        "w_branch_swa": nrm(ks[11], (DEPTH, SWA_WIDTH, D_MODEL), SWA_WIDTH ** -0.5),
        "w_branch_moba": nrm(ks[12], (DEPTH, MOBA_WIDTH, D_MODEL), MOBA_WIDTH ** -0.5),
        "w_branch_gla": nrm(ks[13], (DEPTH, GLA_WIDTH, D_MODEL), GLA_WIDTH ** -0.5),
        "w_out": nrm(ks[14], (DEPTH, D_MODEL, D_MODEL), D_MODEL ** -0.5),
        "ffn_norm": gain(ks[15], (DEPTH, D_MODEL)),
        "w_ffn_gate": nrm(ks[16], (DEPTH, D_MODEL, D_FF), D_MODEL ** -0.5),
        "w_ffn_up": nrm(ks[17], (DEPTH, D_MODEL, D_FF), D_MODEL ** -0.5),
        "w_ffn_down": nrm(ks[18], (DEPTH, D_FF, D_MODEL), D_FF ** -0.5),
    }


def reference(x, positions, attn_norm, w_in, swa_q_norm, swa_k_norm, swa_sinks, moba_q_norm, moba_k_norm,
              gla_w_alpha, gla_b_alpha, gla_out_norm, w_branch_swa, w_branch_moba, w_branch_gla, w_out,
              ffn_norm, w_ffn_gate, w_ffn_up, w_ffn_down):
    cos, sin = rope_tables(positions, HEAD_DIM, x.dtype)
    for l in range(DEPTH):
        h = rms_norm(x, attn_norm[l])
        x = x + hybrid_mixer(h, cos, sin, w_in[l], swa_q_norm[l], swa_k_norm[l], swa_sinks[l],
                             moba_q_norm[l], moba_k_norm[l], gla_w_alpha[l], gla_b_alpha[l],
                             gla_out_norm[l], w_branch_swa[l], w_branch_moba[l], w_branch_gla[l], w_out[l])
        h = rms_norm(x, ffn_norm[l])
        x = x + swiglu(h, w_ffn_gate[l], w_ffn_up[l], w_ffn_down[l])
    return x
```

```python
import functools

import jax
import jax.numpy as jnp
from jax import lax
from jax.experimental import pallas as pl
from jax.experimental.pallas import tpu as pltpu

F32 = jnp.float32
BF16 = jnp.bfloat16

D_MODEL = 1024
HEAD_DIM = 64
ROPE_THETA = 10000.0
EPS = 1e-6
SWA_Q_HEADS = 8
SWA_KV_HEADS = 2
SWA_WINDOW = 128
MOBA_HEADS = 8
MOBA_BLOCK = 256
MOBA_TOPK = 3
GLA_HEADS = 4
GLA_DK = 64
GLA_DV = 128
GLA_GATE_RANK = 16
GLA_TAU = 16.0
GLA_CHUNK = 64
N_BRANCH = 3
D_FF = 2816
QK_SCALE = HEAD_DIM ** -0.5
NEG = -1e30

VMEM_LIMIT_BYTES = 56 * 1024 * 1024
LANES = 128

_C_QA, _C_KA, _C_VA = (0, 512), (512, 640), (640, 768)
_C_QB, _C_KB, _C_VB = (768, 1280), (1280, 1792), (1792, 2304)
_C_QG, _C_KG, _C_VG = (2304, 2560), (2560, 2816), (2816, 3328)
_C_AL, _C_RG, _C_GATES = (3328, 3344), (3344, 3856), (3856, 6928)

FM_QB, FM_KB, FM_VB, FM_QA, FM_KA, FM_VA, FM_KG, FM_AL = 0, 512, 1024, 1536, 2048, 2176, 2304, 2560
FM_ROWS = 2576
TK_GATES, TK_VG, TK_RG, TK_QG, TK_AL = 0, 3072, 3584, 4096, 4352
TK_COLS = 4480

IN_TM = 512
MERGE_TM = 512
FFN_TM = 512
FFN_CHUNK = 256
SWA_TQ = 128
GLA_STEP = 256


def _sigmoid(x):
    return 1.0 / (1.0 + jnp.exp(-x))


def _log_sigmoid(x):
    return jnp.minimum(x, 0.0) - jnp.log(1.0 + jnp.exp(-jnp.abs(x)))


def _iota_div(shape, dim, n):
    assert n & (n - 1) == 0
    return lax.shift_right_logical(lax.broadcasted_iota(jnp.int32, shape, dim), n.bit_length() - 1)


def _split_bf16(x):
    hi = x.astype(BF16)
    lo = (x - hi.astype(F32)).astype(BF16)
    return hi, lo


def _cparams(sem):
    return pltpu.CompilerParams(dimension_semantics=sem, vmem_limit_bytes=VMEM_LIMIT_BYTES)


def _in_proj_kernel(x_ref, pos_ref, invf_ref, g_ref, wtok_ref, wfm_ref, gains_ref, tok_ref, fm_ref, h_scr):
    x = x_ref[0]
    h = x * lax.rsqrt(jnp.mean(x * x, axis=-1, keepdims=True) + EPS) * g_ref[...]
    h_scr[...] = h.astype(BF16)

    step = 512
    for c0 in range(0, TK_COLS, step):
        c1 = min(c0 + step, TK_COLS)
        tok_ref[0, :, c0:c1] = jnp.dot(h_scr[...], wtok_ref[:, c0:c1],
                                       preferred_element_type=F32).astype(BF16)

    ang = invf_ref[:, 0:1] * pos_ref[0].astype(F32)
    cos, sin = jnp.cos(ang), jnp.sin(ang)

    def fm_block(r0, nrows):
        return lax.dot_general(wfm_ref[r0:r0 + nrows, :], h_scr[...], (((1,), (1,)), ((), ())),
                               preferred_element_type=F32)

    def norm_rope(a, gain_col, scale):
        y = a * lax.rsqrt(jnp.mean(a * a, axis=0, keepdims=True) + EPS) * gains_ref[:, gain_col:gain_col + 1]
        y1, y2 = y[:HEAD_DIM // 2], y[HEAD_DIM // 2:]
        out = jnp.concatenate([y1 * cos - y2 * sin, y2 * cos + y1 * sin], axis=0)
        if scale != 1.0:
            out = out * scale
        return out

    blk = 256
    qk_regions = (
        (FM_QB, 512, 0, QK_SCALE), (FM_KB, 512, 1, 1.0), (FM_QA, 512, 2, QK_SCALE), (FM_KA, 128, 3, 1.0))
    for row0, rows, gcol, scale in qk_regions:
        for r0 in range(row0, row0 + rows, blk):
            n = min(blk, row0 + rows - r0)
            a = fm_block(r0, n)
            for hh in range(n // HEAD_DIM):
                fm_ref[0, r0 + hh * HEAD_DIM:r0 + (hh + 1) * HEAD_DIM, :] = norm_rope(
                    a[hh * HEAD_DIM:(hh + 1) * HEAD_DIM], gcol, scale).astype(BF16)
    for row0, rows in ((FM_VB, 512), (FM_VA, 128), (FM_KG, 256), (FM_AL, 16)):
        for r0 in range(row0, row0 + rows, blk):
            n = min(blk, row0 + rows - r0)
            fm_ref[0, r0:r0 + n, :] = fm_block(r0, n).astype(BF16)


def _in_proj(x, pos3, invf, g, wtok, wfm, gains):
    B, S, D = x.shape
    tm = IN_TM
    const = lambda shape: pl.BlockSpec(shape, lambda b, i: (0,) * len(shape))
    return pl.pallas_call(
        _in_proj_kernel,
        grid=(B, S // tm),
        in_specs=[
            pl.BlockSpec((1, tm, D), lambda b, i: (b, i, 0)),
            pl.BlockSpec((1, 1, tm), lambda b, i: (b, 0, i)),
            const((HEAD_DIM // 2, LANES)),
            const((1, D)),
            const((D, TK_COLS)),
            const((FM_ROWS, D)),
            const((HEAD_DIM, LANES)),
        ],
        out_specs=[
            pl.BlockSpec((1, tm, TK_COLS), lambda b, i: (b, i, 0)),
            pl.BlockSpec((1, FM_ROWS, tm), lambda b, i: (b, 0, i)),
        ],
        out_shape=[
            jax.ShapeDtypeStruct((B, S, TK_COLS), BF16),
            jax.ShapeDtypeStruct((B, FM_ROWS, S), BF16),
        ],
        scratch_shapes=[pltpu.VMEM((tm, D), BF16)],
        compiler_params=_cparams(("parallel", "parallel")),
        name="in_proj",
    )(x, pos3, invf, g, wtok, wfm, gains)


def _swa_kernel(q_ref, k_ref, v_ref, sink_ref, o_ref, k_scr):
    i = pl.program_id(1)
    tq = q_ref.shape[2]
    S = k_ref.shape[2]
    grp = SWA_Q_HEADS // SWA_KV_HEADS

    @pl.when(i == 0)
    def _():
        for j in range(S // LANES):
            k_scr[j * LANES:(j + 1) * LANES, :] = (
                k_ref[0, :, j * LANES:(j + 1) * LANES].astype(F32).T.astype(BF16))

    start = pl.multiple_of(jnp.maximum(i - 1, 0) * tq, tq)
    nk = 2 * tq
    kblk = k_scr[pl.ds(start, nk), :]
    kpos = start + lax.broadcasted_iota(jnp.int32, (nk, tq), 0)
    qpos = i * tq + lax.broadcasted_iota(jnp.int32, (nk, tq), 1)
    rel = qpos - kpos
    bias = jnp.where((rel >= 0) & (rel < SWA_WINDOW), 0.0, NEG).astype(F32)

    zeros = jnp.zeros((HEAD_DIM, grp * tq), BF16)
    for g in range(SWA_KV_HEADS):
        qg = jnp.concatenate(
            [q_ref[0, (g * grp + hh) * HEAD_DIM:(g * grp + hh + 1) * HEAD_DIM, :] for hh in range(grp)],
            axis=1)
        z = jnp.concatenate([qg, zeros] if g == 0 else [zeros, qg], axis=0)
        s = jnp.dot(kblk, z, preferred_element_type=F32)
        ps, invs = [], []
        for hh in range(grp):
            h = g * grp + hh
            sh = s[:, hh * tq:(hh + 1) * tq] + bias
            sink = sink_ref[h:h + 1, :]
            m = jnp.maximum(jnp.max(sh, axis=0, keepdims=True), sink)
            p = jnp.exp(sh - m)
            den = jnp.sum(p, axis=0, keepdims=True) + jnp.exp(sink - m)
            ps.append(p.astype(BF16))
            invs.append(1.0 / den)
        pcat = jnp.concatenate(ps, axis=1)
        vblk = v_ref[0, g * HEAD_DIM:(g + 1) * HEAD_DIM, pl.ds(start, nk)]
        o = jnp.dot(vblk, pcat, preferred_element_type=F32)
        for hh in range(grp):
            h = g * grp + hh
            o_ref[0, h * HEAD_DIM:(h + 1) * HEAD_DIM, :] = (
                o[:, hh * tq:(hh + 1) * tq] * invs[hh]).astype(BF16)


def _swa(fm, sinks_b):
    B, _, S = fm.shape
    tq = SWA_TQ
    return pl.pallas_call(
        _swa_kernel,
        grid=(B, S // tq),
        in_specs=[
            pl.BlockSpec((1, 512, tq), lambda b, i: (b, FM_QA // 512, i)),
            pl.BlockSpec((1, 128, S), lambda b, i: (b, FM_KA // 128, 0)),
            pl.BlockSpec((1, 128, S), lambda b, i: (b, FM_VA // 128, 0)),
            pl.BlockSpec((SWA_Q_HEADS, tq), lambda b, i: (0, 0)),
        ],
        out_specs=pl.BlockSpec((1, 512, tq), lambda b, i: (b, 0, i)),
        out_shape=jax.ShapeDtypeStruct((B, 512, S), BF16),
        scratch_shapes=[pltpu.VMEM((S, LANES), BF16)],
        compiler_params=_cparams(("parallel", "arbitrary")),
        name="swa",
    )(fm, fm, fm, sinks_b)


def _moba_kernel(q_ref, k_ref, v_ref, o_ref, k_scr, kmean_scr, bias_scr):
    i = pl.program_id(2)
    L = MOBA_BLOCK
    tq = q_ref.shape[2]
    S = k_ref.shape[2]
    nb = S // L
    nbp = kmean_scr.shape[0]

    @pl.when(i == 0)
    def _():
        for j in range(S // LANES):
            k_scr[j * LANES:(j + 1) * LANES, :] = (
                k_ref[0, :, j * LANES:(j + 1) * LANES].astype(F32).T.astype(BF16))
        blk_of_tok = _iota_div((nbp, S), 1, L)
        row = lax.broadcasted_iota(jnp.int32, (nbp, S), 0)
        sel = jnp.where(blk_of_tok == row, 1.0 / L, 0.0).astype(BF16)
        kmean_scr[...] = jnp.dot(sel, k_scr[...], preferred_element_type=F32)

    q = q_ref[0]
    lane_head = _iota_div((nbp, LANES), 1, HEAD_DIM)
    row_head = _iota_div((LANES, tq), 0, HEAD_DIM)
    i_row = jnp.full((1, tq), i, jnp.int32)

    zs = []
    for hh in range(2):
        km = jnp.where(lane_head == hh, kmean_scr[...], 0.0)
        km_hi, km_lo = _split_bf16(km)
        gate = (jnp.dot(km_hi, q, preferred_element_type=F32)
                + jnp.dot(km_lo, q, preferred_element_type=F32))
        rows = [gate[j:j + 1, :] for j in range(nb)]
        for j in range(nb):
            rank = jnp.zeros((1, tq), jnp.int32)
            for j2 in range(nb):
                if j2 == j:
                    continue
                ahead = (rows[j2] > rows[j]) | (rows[j2] == rows[j]) if j2 < j else (rows[j2] > rows[j])
                rank = rank + jnp.where(ahead & (i_row > j2), 1, 0)
            inc = (rank < MOBA_TOPK) & (i_row > j)
            bias_scr[hh, j:j + 1, :] = jnp.where(inc, 0.0, NEG).astype(F32)
        zs.append(jnp.where(row_head == hh, q, jnp.zeros_like(q)))

    own = pl.multiple_of(i * L, L)
    kblk = k_scr[pl.ds(own, L), :]
    causal = (lax.broadcasted_iota(jnp.int32, (L, tq), 0) <= lax.broadcasted_iota(jnp.int32, (L, tq), 1))
    carry = []
    for hh in range(2):
        s = jnp.where(causal, jnp.dot(kblk, zs[hh], preferred_element_type=F32), NEG)
        m = jnp.max(s, axis=0, keepdims=True)
        p = jnp.exp(s - m)
        l = jnp.sum(p, axis=0, keepdims=True)
        vblk = v_ref[0, hh * HEAD_DIM:(hh + 1) * HEAD_DIM, pl.ds(own, L)]
        acc = jnp.dot(vblk, p.astype(BF16), preferred_element_type=F32)
        carry += [m, l, acc]

    def body(j, carry):
        off = pl.multiple_of(j * L, L)
        kb = k_scr[pl.ds(off, L), :]
        out = []
        for hh in range(2):
            m, l, acc = carry[3 * hh:3 * hh + 3]
            s = jnp.dot(kb, zs[hh], preferred_element_type=F32) + bias_scr[hh, pl.ds(j, 1), :]
            m_new = jnp.maximum(m, jnp.max(s, axis=0, keepdims=True))
            alpha = jnp.exp(m - m_new)
            p = jnp.exp(s - m_new)
            l = alpha * l + jnp.sum(p, axis=0, keepdims=True)
            vb = v_ref[0, hh * HEAD_DIM:(hh + 1) * HEAD_DIM, pl.ds(off, L)]
            acc = alpha * acc + jnp.dot(vb, p.astype(BF16), preferred_element_type=F32)
            out += [m_new, l, acc]
        return tuple(out)

    carry = lax.fori_loop(0, i, body, tuple(carry))
    for hh in range(2):
        m, l, acc = carry[3 * hh:3 * hh + 3]
        o_ref[0, hh * HEAD_DIM:(hh + 1) * HEAD_DIM, :] = (acc * (1.0 / l)).astype(BF16)


def _moba(fm):
    B, _, S = fm.shape
    L = MOBA_BLOCK
    nb = S // L
    nbp = -(-nb // 16) * 16
    hp = MOBA_HEADS // 2
    return pl.pallas_call(
        _moba_kernel,
        grid=(B, hp, nb),
        in_specs=[
            pl.BlockSpec((1, 128, L), lambda b, p, i: (b, FM_QB // 128 + p, i)),
            pl.BlockSpec((1, 128, S), lambda b, p, i: (b, FM_KB // 128 + p, 0)),
            pl.BlockSpec((1, 128, S), lambda b, p, i: (b, FM_VB // 128 + p, 0)),
        ],
        out_specs=pl.BlockSpec((1, 128, L), lambda b, p, i: (b, p, i)),
        out_shape=jax.ShapeDtypeStruct((B, 512, S), BF16),
        scratch_shapes=[pltpu.VMEM((S, LANES), BF16), pltpu.VMEM((nbp, LANES), F32),
                        pltpu.VMEM((2, nbp, L), F32)],
        compiler_params=_cparams(("parallel", "parallel", "arbitrary")),
        name="moba",
    )(fm, fm, fm)


def _gla_kernel(q_ref, kT_ref, v_ref, al_ref, alT_ref, r_ref, wa_ref, waT_ref, ba_ref, baT_ref, gn_ref,
                o_ref, state_scr):
    T = q_ref.shape[1]
    C = GLA_CHUNK
    nc = T // C
    H = GLA_HEADS
    W = H * GLA_DK

    @pl.when(pl.program_id(1) == 0)
    def _():
        state_scr[...] = jnp.zeros_like(state_scr)

    z = jnp.dot(al_ref[0], wa_ref[...], preferred_element_type=F32) + ba_ref[...]
    la = _log_sigmoid(z) * (1.0 / GLA_TAU)
    zT = jnp.dot(waT_ref[...], alT_ref[0], preferred_element_type=F32) + baT_ref[:, 0:1]
    laT = _log_sigmoid(zT) * (1.0 / GLA_TAU)

    ti = lax.broadcasted_iota(jnp.int32, (T, T), 0)
    tj = lax.broadcasted_iota(jnp.int32, (T, T), 1)
    same = _iota_div((T, T), 0, C) == _iota_div((T, T), 1, C)
    tril_bd = same & (tj <= ti)
    low = jnp.where(tril_bd, 1.0, 0.0).astype(BF16)
    upp = jnp.where(same & (ti <= tj), 1.0, 0.0).astype(BF16)
    ones_bd = jnp.where(same, 1.0, 0.0).astype(BF16)
    selc = jnp.where(_iota_div((T, nc * GLA_DV), 0, C) == _iota_div((T, nc * GLA_DV), 1, GLA_DV),
                     1.0, 0.0).astype(BF16)

    la_hi, la_lo = _split_bf16(la)
    b = jnp.dot(low, la_hi, preferred_element_type=F32) + jnp.dot(low, la_lo, preferred_element_type=F32)
    laT_hi, laT_lo = _split_bf16(laT)
    two = lambda rhs: (jnp.dot(laT_hi, rhs, preferred_element_type=F32)
                       + jnp.dot(laT_lo, rhs, preferred_element_type=F32))
    bT = two(upp)
    btotT = two(ones_bd)
    decay = jnp.exp(two(selc))

    qd = (q_ref[0].astype(F32) * QK_SCALE * jnp.exp(b)).astype(BF16)
    kT = kT_ref[0].astype(F32)
    kTd = (kT * jnp.exp(-bT)).astype(BF16)
    kTt = (kT * jnp.exp(btotT - bT)).astype(BF16)

    lane_in_pair = _iota_div((T, 2 * GLA_DK), 1, GLA_DK)
    blkmask = _iota_div((nc * GLA_DK, T), 0, GLA_DK) == _iota_div((nc * GLA_DK, T), 1, C)
    gn = gn_ref[...]

    qms, states = [], []
    for h in range(H):
        p = h // 2
        qp = qd[:, p * 128:(p + 1) * 128]
        qms.append(jnp.where(lane_in_pair == (h % 2), qp, jnp.zeros_like(qp)))
        v_h = v_ref[0, :, h * GLA_DV:(h + 1) * GLA_DV]
        kt_h = kTt[h * GLA_DK:(h + 1) * GLA_DK, :]
        kst = jnp.where(blkmask, jnp.concatenate([kt_h] * nc, axis=0), jnp.zeros((nc * GLA_DK, T), BF16))
        ds_all = jnp.dot(kst, v_h, preferred_element_type=F32)
        st = state_scr[h * GLA_DK:(h + 1) * GLA_DK, :]
        before = []
        for c in range(nc):
            before.append(st.astype(BF16))
            st = (decay[h * GLA_DK:(h + 1) * GLA_DK, c * GLA_DV:(c + 1) * GLA_DV] * st
                  + ds_all[c * GLA_DK:(c + 1) * GLA_DK])
        state_scr[h * GLA_DK:(h + 1) * GLA_DK, :] = st
        states.append(before)

    for h in range(H):
        p = h // 2
        v_h = v_ref[0, :, h * GLA_DV:(h + 1) * GLA_DV]
        a = jnp.dot(qms[h], kTd[p * 128:(p + 1) * 128, :], preferred_element_type=F32)
        a = jnp.where(tril_bd, a, 0.0).astype(BF16)
        o = jnp.dot(a, v_h, preferred_element_type=F32)
        rhs = jnp.concatenate(
            [jnp.concatenate([states[2 * p][c], states[2 * p + 1][c]], axis=0) for c in range(nc)], axis=1)
        oi = jnp.dot(qms[h], rhs, preferred_element_type=F32)
        o = o + jnp.concatenate(
            [oi[c * C:(c + 1) * C, c * GLA_DV:(c + 1) * GLA_DV] for c in range(nc)], axis=0)
        y = o * lax.rsqrt(jnp.mean(o * o, axis=-1, keepdims=True) + EPS) * gn
        r = r_ref[0, :, h * GLA_DV:(h + 1) * GLA_DV].astype(F32)
        o_ref[0, :, h * GLA_DV:(h + 1) * GLA_DV] = (y * (r * _sigmoid(r))).astype(BF16)


def _gla(tok, fm, wa, waT, ba, baT, gn):
    B, S, _ = tok.shape
    T = GLA_STEP
    const = lambda shape: pl.BlockSpec(shape, lambda b, s: (0,) * len(shape))
    return pl.pallas_call(
        _gla_kernel,
        grid=(B, S // T),
        in_specs=[
            pl.BlockSpec((1, T, 256), lambda b, s: (b, s, TK_QG // 256)),
            pl.BlockSpec((1, 256, T), lambda b, s: (b, FM_KG // 256, s)),
            pl.BlockSpec((1, T, 512), lambda b, s: (b, s, TK_VG // 512)),
            pl.BlockSpec((1, T, 128), lambda b, s: (b, s, TK_AL // 128)),
            pl.BlockSpec((1, 16, T), lambda b, s: (b, FM_AL // 16, s)),
            pl.BlockSpec((1, T, 512), lambda b, s: (b, s, TK_RG // 512)),
            const((128, 256)), const((256, 16)), const((1, 256)), const((256, LANES)), const((1, GLA_DV)),
        ],
        out_specs=pl.BlockSpec((1, T, 512), lambda b, s: (b, s, 0)),
        out_shape=jax.ShapeDtypeStruct((B, S, 512), BF16),
        scratch_shapes=[pltpu.VMEM((GLA_HEADS * GLA_DK, GLA_DV), F32)],
        compiler_params=_cparams(("parallel", "arbitrary")),
        name="gla",
    )(tok, fm, tok, tok, fm, tok, wa, waT, ba, baT, gn)


def _merge_kernel(x_ref, yaT_ref, ybT_ref, yc_ref, g_ref, ws_ref, wm_ref, wg_ref, wo_ref, o_ref):
    tn = (((0,), (0,)), ((), ()))
    pa = lax.dot_general(yaT_ref[0], ws_ref[...], tn, preferred_element_type=F32)
    pb = lax.dot_general(ybT_ref[0], wm_ref[...], tn, preferred_element_type=F32)
    pc = jnp.dot(yc_ref[0], wg_ref[...], preferred_element_type=F32)
    D = D_MODEL
    merged = (_sigmoid(g_ref[0, :, 0:D].astype(F32)) * pa
              + _sigmoid(g_ref[0, :, D:2 * D].astype(F32)) * pb
              + _sigmoid(g_ref[0, :, 2 * D:3 * D].astype(F32)) * pc)
    o_ref[0] = x_ref[0] + jnp.dot(merged.astype(BF16), wo_ref[...], preferred_element_type=F32)


def _merge(x, yaT, ybT, yc, tok, ws, wm, wg, wo):
    B, S, D = x.shape
    tm = MERGE_TM
    const = lambda shape: pl.BlockSpec(shape, lambda b, i: (0,) * len(shape))
    return pl.pallas_call(
        _merge_kernel,
        grid=(B, S // tm),
        in_specs=[
            pl.BlockSpec((1, tm, D), lambda b, i: (b, i, 0)),
            pl.BlockSpec((1, 512, tm), lambda b, i: (b, 0, i)),
            pl.BlockSpec((1, 512, tm), lambda b, i: (b, 0, i)),
            pl.BlockSpec((1, tm, 512), lambda b, i: (b, i, 0)),
            pl.BlockSpec((1, tm, N_BRANCH * D), lambda b, i: (b, i, TK_GATES)),
            const((512, D)), const((512, D)), const((512, D)), const((D, D)),
        ],
        out_specs=pl.BlockSpec((1, tm, D), lambda b, i: (b, i, 0)),
        out_shape=jax.ShapeDtypeStruct((B, S, D), F32),
        compiler_params=_cparams(("parallel", "parallel")),
        name="merge",
    )(x, yaT, ybT, yc, tok, ws, wm, wg, wo)


def _ffn_kernel(x_ref, g_ref, wg_ref, wu_ref, wd_ref, o_ref, h_scr):
    x = x_ref[0]
    h = x * lax.rsqrt(jnp.mean(x * x, axis=-1, keepdims=True) + EPS) * g_ref[...]
    h_scr[...] = h.astype(BF16)
    acc = x
    for c0 in range(0, D_FF, FFN_CHUNK):
        c1 = c0 + FFN_CHUNK
        gt = jnp.dot(h_scr[...], wg_ref[:, c0:c1], preferred_element_type=F32)
        up = jnp.dot(h_scr[...], wu_ref[:, c0:c1], preferred_element_type=F32)
        a = (gt * _sigmoid(gt) * up).astype(BF16)
        acc = acc + jnp.dot(a, wd_ref[c0:c1, :], preferred_element_type=F32)
    o_ref[0] = acc


def _ffn(x, g, wg, wu, wd):
    B, S, D = x.shape
    tm = FFN_TM
    const = lambda shape: pl.BlockSpec(shape, lambda b, i: (0,) * len(shape))
    return pl.pallas_call(
        _ffn_kernel,
        grid=(B, S // tm),
        in_specs=[
            pl.BlockSpec((1, tm, D), lambda b, i: (b, i, 0)),
            const((1, D)), const((D, D_FF)), const((D, D_FF)), const((D_FF, D)),
        ],
        out_specs=pl.BlockSpec((1, tm, D), lambda b, i: (b, i, 0)),
        out_shape=jax.ShapeDtypeStruct((B, S, D), F32),
        scratch_shapes=[pltpu.VMEM((tm, D), BF16)],
        compiler_params=_cparams(("parallel", "parallel")),
        name="ffn",
    )(x, g, wg, wu, wd)


def _cols(w, *ranges):
    return jnp.concatenate([w[:, a:b] for a, b in ranges], axis=1)


def _layer_weights(w_in, swa_qn, swa_kn, moba_qn, moba_kn, w_alpha, b_alpha):
    wfm = _cols(w_in, _C_QB, _C_KB, _C_VB, _C_QA, _C_KA, _C_VA, _C_KG, _C_AL).T.astype(BF16)
    al_pad = jnp.pad(w_in[:, _C_AL[0]:_C_AL[1]], ((0, 0), (0, LANES - GLA_GATE_RANK)))
    wtok = jnp.concatenate([_cols(w_in, _C_GATES, _C_VG, _C_RG, _C_QG), al_pad], axis=1).astype(BF16)
    gains = jnp.pad(jnp.stack([moba_qn, moba_kn, swa_qn, swa_kn], axis=1), ((0, 0), (0, LANES - 4)))
    wa = jnp.pad(w_alpha, ((0, LANES - GLA_GATE_RANK), (0, 0))).astype(BF16)
    waT = w_alpha.T.astype(BF16)
    ba = b_alpha.reshape(1, -1)
    baT = jnp.broadcast_to(b_alpha.reshape(-1, 1), (b_alpha.shape[0], LANES))
    return wfm, wtok, gains, wa, waT, ba, baT


def kernel(x, positions, attn_norm, w_in, swa_q_norm, swa_k_norm, swa_sinks, moba_q_norm, moba_k_norm,
           gla_w_alpha, gla_b_alpha, gla_out_norm, w_branch_swa, w_branch_moba, w_branch_gla, w_out,
           ffn_norm, w_ffn_gate, w_ffn_up, w_ffn_down):
    B, S, D = x.shape
    depth = w_in.shape[0]
    pos3 = positions.reshape(B, 1, S)
    inv_freq = ROPE_THETA ** (-jnp.arange(0, HEAD_DIM, 2, dtype=F32) / HEAD_DIM)
    invf = jnp.broadcast_to(inv_freq.reshape(-1, 1), (HEAD_DIM // 2, LANES))
    for l in range(depth):
        wfm, wtok, gains, wa, waT, ba, baT = _layer_weights(
            w_in[l], swa_q_norm[l], swa_k_norm[l], moba_q_norm[l], moba_k_norm[l],
            gla_w_alpha[l], gla_b_alpha[l])
        tok, fm = _in_proj(x, pos3, invf, attn_norm[l].reshape(1, D), wtok, wfm, gains)
        sinks_b = jnp.broadcast_to(swa_sinks[l].reshape(-1, 1), (SWA_Q_HEADS, SWA_TQ))
        yaT = _swa(fm, sinks_b)
        ybT = _moba(fm)
        yc = _gla(tok, fm, wa, waT, ba, baT, gla_out_norm[l].reshape(1, GLA_DV))
        x = _merge(x, yaT, ybT, yc, tok, w_branch_swa[l].astype(BF16), w_branch_moba[l].astype(BF16),
                   w_branch_gla[l].astype(BF16), w_out[l].astype(BF16))
        x = _ffn(x, ffn_norm[l].reshape(1, D), w_ffn_gate[l].astype(BF16), w_ffn_up[l].astype(BF16),
                 w_ffn_down[l].astype(BF16))
    return x
```

```python
import functools

import jax
import jax.numpy as jnp
from jax import lax
from jax.experimental import pallas as pl
from jax.experimental.pallas import tpu as pltpu

F32 = jnp.float32
BF16 = jnp.bfloat16

D_MODEL = 1024
HEAD_DIM = 64
ROPE_THETA = 10000.0
EPS = 1e-6
SWA_Q_HEADS = 8
SWA_KV_HEADS = 2
SWA_WINDOW = 128
MOBA_HEADS = 8
MOBA_BLOCK = 256
MOBA_TOPK = 3
GLA_HEADS = 4
GLA_DK = 64
GLA_DV = 128
GLA_GATE_RANK = 16
GLA_TAU = 16.0
GLA_CHUNK = 64
N_BRANCH = 3
D_FF = 2816
QK_SCALE = HEAD_DIM ** -0.5
NEG = -1e30

VMEM_LIMIT_BYTES = 56 * 1024 * 1024
LANES = 128

_C_QA, _C_KA, _C_VA = (0, 512), (512, 640), (640, 768)
_C_QB, _C_KB, _C_VB = (768, 1280), (1280, 1792), (1792, 2304)
_C_QG, _C_KG, _C_VG = (2304, 2560), (2560, 2816), (2816, 3328)
_C_AL, _C_RG, _C_GATES = (3328, 3344), (3344, 3856), (3856, 6928)

FM_QB, FM_KB, FM_VB, FM_QA, FM_KA, FM_VA, FM_KG, FM_AL = 0, 512, 1024, 1536, 2048, 2176, 2304, 2560
FM_ROWS = 2576
TK_GATES, TK_VG, TK_RG, TK_QG, TK_AL = 0, 3072, 3584, 4096, 4352
TK_COLS = 4480

IN_TM = 512
MERGE_TM = 512
FFN_TM = 512
FFN_CHUNK = 256
SWA_TQ = 128
GLA_STEP = 256


def _sigmoid(x):
    return 1.0 / (1.0 + jnp.exp(-x))


def _log_sigmoid(x):
    return jnp.minimum(x, 0.0) - jnp.log(1.0 + jnp.exp(-jnp.abs(x)))


def _iota_div(shape, dim, n):
    assert n & (n - 1) == 0
    return lax.shift_right_logical(lax.broadcasted_iota(jnp.int32, shape, dim), n.bit_length() - 1)


def _split_bf16(x):
    hi = x.astype(BF16)
    lo = (x - hi.astype(F32)).astype(BF16)
    return hi, lo


def _cparams(sem):
    return pltpu.CompilerParams(dimension_semantics=sem, vmem_limit_bytes=VMEM_LIMIT_BYTES)


def _in_proj_kernel(x_ref, pos_ref, invf_ref, g_ref, wtok_ref, wfm_ref, gains_ref, tok_ref, fm_ref, h_scr):
    x = x_ref[0]
    h = x * lax.rsqrt(jnp.mean(x * x, axis=-1, keepdims=True) + EPS) * g_ref[...]
    h_scr[...] = h.astype(BF16)

    step = 512
    for c0 in range(0, TK_COLS, step):
        c1 = min(c0 + step, TK_COLS)
        tok_ref[0, :, c0:c1] = jnp.dot(h_scr[...], wtok_ref[:, c0:c1],
                                       preferred_element_type=F32).astype(BF16)

    ang = invf_ref[:, 0:1] * pos_ref[0].astype(F32)
    cos, sin = jnp.cos(ang), jnp.sin(ang)

    def fm_block(r0, nrows):
        return lax.dot_general(wfm_ref[r0:r0 + nrows, :], h_scr[...], (((1,), (1,)), ((), ())),
                               preferred_element_type=F32)

    def norm_rope(a, gain_col, scale):
        y = a * lax.rsqrt(jnp.mean(a * a, axis=0, keepdims=True) + EPS) * gains_ref[:, gain_col:gain_col + 1]
        y1, y2 = y[:HEAD_DIM // 2], y[HEAD_DIM // 2:]
        out = jnp.concatenate([y1 * cos - y2 * sin, y2 * cos + y1 * sin], axis=0)
        if scale != 1.0:
            out = out * scale
        return out

    blk = 256
    qk_regions = (
        (FM_QB, 512, 0, QK_SCALE), (FM_KB, 512, 1, 1.0), (FM_QA, 512, 2, QK_SCALE), (FM_KA, 128, 3, 1.0))
    for row0, rows, gcol, scale in qk_regions:
        for r0 in range(row0, row0 + rows, blk):
            n = min(blk, row0 + rows - r0)
            a = fm_block(r0, n)
            for hh in range(n // HEAD_DIM):
                fm_ref[0, r0 + hh * HEAD_DIM:r0 + (hh + 1) * HEAD_DIM, :] = norm_rope(
                    a[hh * HEAD_DIM:(hh + 1) * HEAD_DIM], gcol, scale).astype(BF16)
    for row0, rows in ((FM_VB, 512), (FM_VA, 128), (FM_KG, 256), (FM_AL, 16)):
        for r0 in range(row0, row0 + rows, blk):
            n = min(blk, row0 + rows - r0)
            fm_ref[0, r0:r0 + n, :] = fm_block(r0, n).astype(BF16)


def _in_proj(x, pos3, invf, g, wtok, wfm, gains):
    B, S, D = x.shape
    tm = IN_TM
    const = lambda shape: pl.BlockSpec(shape, lambda b, i: (0,) * len(shape))
    return pl.pallas_call(
        _in_proj_kernel,
        grid=(B, S // tm),
        in_specs=[
            pl.BlockSpec((1, tm, D), lambda b, i: (b, i, 0)),
            pl.BlockSpec((1, 1, tm), lambda b, i: (b, 0, i)),
            const((HEAD_DIM // 2, LANES)),
            const((1, D)),
            const((D, TK_COLS)),
            const((FM_ROWS, D)),
            const((HEAD_DIM, LANES)),
        ],
        out_specs=[
            pl.BlockSpec((1, tm, TK_COLS), lambda b, i: (b, i, 0)),
            pl.BlockSpec((1, FM_ROWS, tm), lambda b, i: (b, 0, i)),
        ],
        out_shape=[
            jax.ShapeDtypeStruct((B, S, TK_COLS), BF16),
            jax.ShapeDtypeStruct((B, FM_ROWS, S), BF16),
        ],
        scratch_shapes=[pltpu.VMEM((tm, D), BF16)],
        compiler_params=_cparams(("parallel", "parallel")),
        name="in_proj",
    )(x, pos3, invf, g, wtok, wfm, gains)


def _swa_kernel(q_ref, k_ref, v_ref, sink_ref, o_ref, k_scr):
    i = pl.program_id(1)
    tq = q_ref.shape[2]
    S = k_ref.shape[2]
    grp = SWA_Q_HEADS // SWA_KV_HEADS

    @pl.when(i == 0)
    def _():
        for j in range(S // LANES):
            k_scr[j * LANES:(j + 1) * LANES, :] = (
                k_ref[0, :, j * LANES:(j + 1) * LANES].astype(F32).T.astype(BF16))

    start = pl.multiple_of(jnp.maximum(i - 1, 0) * tq, tq)
    nk = 2 * tq
    kblk = k_scr[pl.ds(start, nk), :]
    kpos = start + lax.broadcasted_iota(jnp.int32, (nk, tq), 0)
    qpos = i * tq + lax.broadcasted_iota(jnp.int32, (nk, tq), 1)
    rel = qpos - kpos
    bias = jnp.where((rel >= 0) & (rel < SWA_WINDOW), 0.0, NEG).astype(F32)

    zeros = jnp.zeros((HEAD_DIM, grp * tq), BF16)
    for g in range(SWA_KV_HEADS):
        qg = jnp.concatenate(
            [q_ref[0, (g * grp + hh) * HEAD_DIM:(g * grp + hh + 1) * HEAD_DIM, :] for hh in range(grp)],
            axis=1)
        z = jnp.concatenate([qg, zeros] if g == 0 else [zeros, qg], axis=0)
        s = jnp.dot(kblk, z, preferred_element_type=F32)
        ps, invs = [], []
        for hh in range(grp):
            h = g * grp + hh
            sh = s[:, hh * tq:(hh + 1) * tq] + bias
            sink = sink_ref[h:h + 1, :]
            m = jnp.maximum(jnp.max(sh, axis=0, keepdims=True), sink)
            p = jnp.exp(sh - m)
            den = jnp.sum(p, axis=0, keepdims=True) + jnp.exp(sink - m)
            ps.append(p.astype(BF16))
            invs.append(1.0 / den)
        pcat = jnp.concatenate(ps, axis=1)
        vblk = v_ref[0, g * HEAD_DIM:(g + 1) * HEAD_DIM, pl.ds(start, nk)]
        o = jnp.dot(vblk, pcat, preferred_element_type=F32)
        for hh in range(grp):
            h = g * grp + hh
            o_ref[0, h * HEAD_DIM:(h + 1) * HEAD_DIM, :] = (
                o[:, hh * tq:(hh + 1) * tq] * invs[hh]).astype(BF16)


def _swa(fm, sinks_b):
    B, _, S = fm.shape
    tq = SWA_TQ
    return pl.pallas_call(
        _swa_kernel,
        grid=(B, S // tq),
        in_specs=[
            pl.BlockSpec((1, 512, tq), lambda b, i: (b, FM_QA // 512, i)),
            pl.BlockSpec((1, 128, S), lambda b, i: (b, FM_KA // 128, 0)),
            pl.BlockSpec((1, 128, S), lambda b, i: (b, FM_VA // 128, 0)),
            pl.BlockSpec((SWA_Q_HEADS, tq), lambda b, i: (0, 0)),
        ],
        out_specs=pl.BlockSpec((1, 512, tq), lambda b, i: (b, 0, i)),
        out_shape=jax.ShapeDtypeStruct((B, 512, S), BF16),
        scratch_shapes=[pltpu.VMEM((S, LANES), BF16)],
        compiler_params=_cparams(("parallel", "arbitrary")),
        name="swa",
    )(fm, fm, fm, sinks_b)


def _moba_kernel(q_ref, k_ref, v_ref, o_ref, k_scr):
    L = MOBA_BLOCK
    S = k_ref.shape[2]
    nb = S // L
    assert nb == 8
    nbp = 16

    head_row = _iota_div((LANES, LANES), 0, HEAD_DIM)
    for j in range(S // LANES):
        kt = k_ref[0, :, j * LANES:(j + 1) * LANES].astype(F32)
        for hh in range(2):
            k_scr[hh, j * LANES:(j + 1) * LANES, :] = jnp.where(head_row == hh, kt, 0.0).T.astype(BF16)

    sel = jnp.where(_iota_div((nbp, S), 1, L) == lax.broadcasted_iota(jnp.int32, (nbp, S), 0),
                    1.0 / L, 0.0).astype(BF16)
    q = q_ref[0]
    jrow = lax.broadcasted_iota(jnp.int32, (nb, S), 0)
    qblk = _iota_div((nb, S), 1, L)
    causal = (lax.broadcasted_iota(jnp.int32, (L, L), 0) <= lax.broadcasted_iota(jnp.int32, (L, L), 1))

    for hh in range(2):
        kmean = jnp.dot(sel, k_scr[hh], preferred_element_type=F32)
        km_hi, km_lo = _split_bf16(kmean)
        gate = (jnp.dot(km_hi, q, preferred_element_type=F32)
                + jnp.dot(km_lo, q, preferred_element_type=F32))[:nb]
        rank = jnp.zeros((nb, S), jnp.int32)
        for d in range(1, nb):
            other = pltpu.roll(gate, d, axis=0)
            wrapped = jrow < d
            j2 = jnp.where(wrapped, jrow - d + nb, jrow - d)
            ahead = (other > gate) | ((other == gate) & jnp.logical_not(wrapped))
            rank = rank + jnp.where(ahead & (qblk > j2), 1, 0)
        bias = jnp.where((rank < MOBA_TOPK) & (qblk > jrow), 0.0, NEG).astype(F32)

        vT = v_ref.at[0, hh * HEAD_DIM:(hh + 1) * HEAD_DIM, :]
        for i in range(nb):
            cols = slice(i * L, (i + 1) * L)
            s_all = jnp.dot(k_scr[hh, 0:(i + 1) * L, :], q[:, cols], preferred_element_type=F32)
            s_own = jnp.where(causal, s_all[i * L:(i + 1) * L], NEG)
            m = jnp.max(s_own, axis=0, keepdims=True)
            past = []
            for j in range(i):
                s_j = s_all[j * L:(j + 1) * L]
                b_j = bias[j:j + 1, cols]
                m = jnp.maximum(m, jnp.max(s_j, axis=0, keepdims=True) + b_j)
                past.append((s_j, b_j))
            p = jnp.exp(s_own - m)
            l = jnp.sum(p, axis=0, keepdims=True)
            acc = jnp.dot(vT[:, cols], p.astype(BF16), preferred_element_type=F32)
            for j, (s_j, b_j) in enumerate(past):
                p = jnp.exp(s_j + (b_j - m))
                l = l + jnp.sum(p, axis=0, keepdims=True)
                acc = acc + jnp.dot(vT[:, j * L:(j + 1) * L], p.astype(BF16), preferred_element_type=F32)
            o_ref[0, hh * HEAD_DIM:(hh + 1) * HEAD_DIM, cols] = (acc * (1.0 / l)).astype(BF16)


def _moba(fm):
    B, _, S = fm.shape
    hp = MOBA_HEADS // 2
    return pl.pallas_call(
        _moba_kernel,
        grid=(B, hp),
        in_specs=[
            pl.BlockSpec((1, 128, S), lambda b, p: (b, FM_QB // 128 + p, 0)),
            pl.BlockSpec((1, 128, S), lambda b, p: (b, FM_KB // 128 + p, 0)),
            pl.BlockSpec((1, 128, S), lambda b, p: (b, FM_VB // 128 + p, 0)),
        ],
        out_specs=pl.BlockSpec((1, 128, S), lambda b, p: (b, p, 0)),
        out_shape=jax.ShapeDtypeStruct((B, 512, S), BF16),
        scratch_shapes=[pltpu.VMEM((2, S, LANES), BF16)],
        compiler_params=_cparams(("parallel", "parallel")),
        name="moba",
    )(fm, fm, fm)


def _gla_kernel(q_ref, kT_ref, v_ref, al_ref, alT_ref, r_ref, wa_ref, waT_ref, ba_ref, baT_ref, gn_ref,
                o_ref, state_scr):
    T = q_ref.shape[1]
    C = GLA_CHUNK
    nc = T // C
    H = GLA_HEADS
    W = H * GLA_DK

    @pl.when(pl.program_id(1) == 0)
    def _():
        state_scr[...] = jnp.zeros_like(state_scr)

    z = jnp.dot(al_ref[0], wa_ref[...], preferred_element_type=F32) + ba_ref[...]
    la = _log_sigmoid(z) * (1.0 / GLA_TAU)
    zT = jnp.dot(waT_ref[...], alT_ref[0], preferred_element_type=F32) + baT_ref[:, 0:1]
    laT = _log_sigmoid(zT) * (1.0 / GLA_TAU)

    ti = lax.broadcasted_iota(jnp.int32, (T, T), 0)
    tj = lax.broadcasted_iota(jnp.int32, (T, T), 1)
    same = _iota_div((T, T), 0, C) == _iota_div((T, T), 1, C)
    tril_bd = same & (tj <= ti)
    low = jnp.where(tril_bd, 1.0, 0.0).astype(BF16)
    upp = jnp.where(same & (ti <= tj), 1.0, 0.0).astype(BF16)
    ones_bd = jnp.where(same, 1.0, 0.0).astype(BF16)
    selc = jnp.where(_iota_div((T, nc * GLA_DV), 0, C) == _iota_div((T, nc * GLA_DV), 1, GLA_DV),
                     1.0, 0.0).astype(BF16)

    la_hi, la_lo = _split_bf16(la)
    b = jnp.dot(low, la_hi, preferred_element_type=F32) + jnp.dot(low, la_lo, preferred_element_type=F32)
    laT_hi, laT_lo = _split_bf16(laT)
    two = lambda rhs: (jnp.dot(laT_hi, rhs, preferred_element_type=F32)
                       + jnp.dot(laT_lo, rhs, preferred_element_type=F32))
    bT = two(upp)
    btotT = two(ones_bd)
    decay = jnp.exp(two(selc))

    qd = (q_ref[0].astype(F32) * QK_SCALE * jnp.exp(b)).astype(BF16)
    kT = kT_ref[0].astype(F32)
    kTd = (kT * jnp.exp(-bT)).astype(BF16)
    kTt = (kT * jnp.exp(btotT - bT)).astype(BF16)

    lane_in_pair = _iota_div((T, 2 * GLA_DK), 1, GLA_DK)
    blkmask = _iota_div((nc * GLA_DK, T), 0, GLA_DK) == _iota_div((nc * GLA_DK, T), 1, C)
    gn = gn_ref[...]

    qms, states = [], []
    for h in range(H):
        p = h // 2
        qp = qd[:, p * 128:(p + 1) * 128]
        qms.append(jnp.where(lane_in_pair == (h % 2), qp, jnp.zeros_like(qp)))
        v_h = v_ref[0, :, h * GLA_DV:(h + 1) * GLA_DV]
        kt_h = kTt[h * GLA_DK:(h + 1) * GLA_DK, :]
        kst = jnp.where(blkmask, jnp.concatenate([kt_h] * nc, axis=0), jnp.zeros((nc * GLA_DK, T), BF16))
        ds_all = jnp.dot(kst, v_h, preferred_element_type=F32)
        st = state_scr[h * GLA_DK:(h + 1) * GLA_DK, :]
        before = []
        for c in range(nc):
            before.append(st.astype(BF16))
            st = (decay[h * GLA_DK:(h + 1) * GLA_DK, c * GLA_DV:(c + 1) * GLA_DV] * st
                  + ds_all[c * GLA_DK:(c + 1) * GLA_DK])
        state_scr[h * GLA_DK:(h + 1) * GLA_DK, :] = st
        states.append(before)

    for h in range(H):
        p = h // 2
        v_h = v_ref[0, :, h * GLA_DV:(h + 1) * GLA_DV]
        a = jnp.dot(qms[h], kTd[p * 128:(p + 1) * 128, :], preferred_element_type=F32)
        a = jnp.where(tril_bd, a, 0.0).astype(BF16)
        o = jnp.dot(a, v_h, preferred_element_type=F32)
        rhs = jnp.concatenate(
            [jnp.concatenate([states[2 * p][c], states[2 * p + 1][c]], axis=0) for c in range(nc)], axis=1)
        oi = jnp.dot(qms[h], rhs, preferred_element_type=F32)
        o = o + jnp.concatenate(
            [oi[c * C:(c + 1) * C, c * GLA_DV:(c + 1) * GLA_DV] for c in range(nc)], axis=0)
        y = o * lax.rsqrt(jnp.mean(o * o, axis=-1, keepdims=True) + EPS) * gn
        r = r_ref[0, :, h * GLA_DV:(h + 1) * GLA_DV].astype(F32)
        o_ref[0, :, h * GLA_DV:(h + 1) * GLA_DV] = (y * (r * _sigmoid(r))).astype(BF16)


def _gla(tok, fm, wa, waT, ba, baT, gn):
    B, S, _ = tok.shape
    T = GLA_STEP
    const = lambda shape: pl.BlockSpec(shape, lambda b, s: (0,) * len(shape))
    return pl.pallas_call(
        _gla_kernel,
        grid=(B, S // T),
        in_specs=[
            pl.BlockSpec((1, T, 256), lambda b, s: (b, s, TK_QG // 256)),
            pl.BlockSpec((1, 256, T), lambda b, s: (b, FM_KG // 256, s)),
            pl.BlockSpec((1, T, 512), lambda b, s: (b, s, TK_VG // 512)),
            pl.BlockSpec((1, T, 128), lambda b, s: (b, s, TK_AL // 128)),
            pl.BlockSpec((1, 16, T), lambda b, s: (b, FM_AL // 16, s)),
            pl.BlockSpec((1, T, 512), lambda b, s: (b, s, TK_RG // 512)),
            const((128, 256)), const((256, 16)), const((1, 256)), const((256, LANES)), const((1, GLA_DV)),
        ],
        out_specs=pl.BlockSpec((1, T, 512), lambda b, s: (b, s, 0)),
        out_shape=jax.ShapeDtypeStruct((B, S, 512), BF16),
        scratch_shapes=[pltpu.VMEM((GLA_HEADS * GLA_DK, GLA_DV), F32)],
        compiler_params=_cparams(("parallel", "arbitrary")),
        name="gla",
    )(tok, fm, tok, tok, fm, tok, wa, waT, ba, baT, gn)


def _merge_kernel(x_ref, yaT_ref, ybT_ref, yc_ref, g_ref, ws_ref, wm_ref, wg_ref, wo_ref, o_ref):
    tn = (((0,), (0,)), ((), ()))
    pa = lax.dot_general(yaT_ref[0], ws_ref[...], tn, preferred_element_type=F32)
    pb = lax.dot_general(ybT_ref[0], wm_ref[...], tn, preferred_element_type=F32)
    pc = jnp.dot(yc_ref[0], wg_ref[...], preferred_element_type=F32)
    D = D_MODEL
    merged = (_sigmoid(g_ref[0, :, 0:D].astype(F32)) * pa
              + _sigmoid(g_ref[0, :, D:2 * D].astype(F32)) * pb
              + _sigmoid(g_ref[0, :, 2 * D:3 * D].astype(F32)) * pc)
    o_ref[0] = x_ref[0] + jnp.dot(merged.astype(BF16), wo_ref[...], preferred_element_type=F32)


def _merge(x, yaT, ybT, yc, tok, ws, wm, wg, wo):
    B, S, D = x.shape
    tm = MERGE_TM
    const = lambda shape: pl.BlockSpec(shape, lambda b, i: (0,) * len(shape))
    return pl.pallas_call(
        _merge_kernel,
        grid=(B, S // tm),
        in_specs=[
            pl.BlockSpec((1, tm, D), lambda b, i: (b, i, 0)),
            pl.BlockSpec((1, 512, tm), lambda b, i: (b, 0, i)),
            pl.BlockSpec((1, 512, tm), lambda b, i: (b, 0, i)),
            pl.BlockSpec((1, tm, 512), lambda b, i: (b, i, 0)),
            pl.BlockSpec((1, tm, N_BRANCH * D), lambda b, i: (b, i, TK_GATES)),
            const((512, D)), const((512, D)), const((512, D)), const((D, D)),
        ],
        out_specs=pl.BlockSpec((1, tm, D), lambda b, i: (b, i, 0)),
        out_shape=jax.ShapeDtypeStruct((B, S, D), F32),
        compiler_params=_cparams(("parallel", "parallel")),
        name="merge",
    )(x, yaT, ybT, yc, tok, ws, wm, wg, wo)


def _ffn_kernel(x_ref, g_ref, wg_ref, wu_ref, wd_ref, o_ref, h_scr):
    x = x_ref[0]
    h = x * lax.rsqrt(jnp.mean(x * x, axis=-1, keepdims=True) + EPS) * g_ref[...]
    h_scr[...] = h.astype(BF16)
    acc = x
    for c0 in range(0, D_FF, FFN_CHUNK):
        c1 = c0 + FFN_CHUNK
        gt = jnp.dot(h_scr[...], wg_ref[:, c0:c1], preferred_element_type=F32)
        up = jnp.dot(h_scr[...], wu_ref[:, c0:c1], preferred_element_type=F32)
        a = (gt * _sigmoid(gt) * up).astype(BF16)
        acc = acc + jnp.dot(a, wd_ref[c0:c1, :], preferred_element_type=F32)
    o_ref[0] = acc


def _ffn(x, g, wg, wu, wd):
    B, S, D = x.shape
    tm = FFN_TM
    const = lambda shape: pl.BlockSpec(shape, lambda b, i: (0,) * len(shape))
    return pl.pallas_call(
        _ffn_kernel,
        grid=(B, S // tm),
        in_specs=[
            pl.BlockSpec((1, tm, D), lambda b, i: (b, i, 0)),
            const((1, D)), const((D, D_FF)), const((D, D_FF)), const((D_FF, D)),
        ],
        out_specs=pl.BlockSpec((1, tm, D), lambda b, i: (b, i, 0)),
        out_shape=jax.ShapeDtypeStruct((B, S, D), F32),
        scratch_shapes=[pltpu.VMEM((tm, D), BF16)],
        compiler_params=_cparams(("parallel", "parallel")),
        name="ffn",
    )(x, g, wg, wu, wd)


def _cols(w, *ranges):
    return jnp.concatenate([w[:, a:b] for a, b in ranges], axis=1)


def _layer_weights(w_in, swa_qn, swa_kn, moba_qn, moba_kn, w_alpha, b_alpha):
    wfm = _cols(w_in, _C_QB, _C_KB, _C_VB, _C_QA, _C_KA, _C_VA, _C_KG, _C_AL).T.astype(BF16)
    al_pad = jnp.pad(w_in[:, _C_AL[0]:_C_AL[1]], ((0, 0), (0, LANES - GLA_GATE_RANK)))
    wtok = jnp.concatenate([_cols(w_in, _C_GATES, _C_VG, _C_RG, _C_QG), al_pad], axis=1).astype(BF16)
    gains = jnp.pad(jnp.stack([moba_qn, moba_kn, swa_qn, swa_kn], axis=1), ((0, 0), (0, LANES - 4)))
    wa = jnp.pad(w_alpha, ((0, LANES - GLA_GATE_RANK), (0, 0))).astype(BF16)
    waT = w_alpha.T.astype(BF16)
    ba = b_alpha.reshape(1, -1)
    baT = jnp.broadcast_to(b_alpha.reshape(-1, 1), (b_alpha.shape[0], LANES))
    return wfm, wtok, gains, wa, waT, ba, baT


def kernel(x, positions, attn_norm, w_in, swa_q_norm, swa_k_norm, swa_sinks, moba_q_norm, moba_k_norm,
           gla_w_alpha, gla_b_alpha, gla_out_norm, w_branch_swa, w_branch_moba, w_branch_gla, w_out,
           ffn_norm, w_ffn_gate, w_ffn_up, w_ffn_down):
    B, S, D = x.shape
    depth = w_in.shape[0]
    pos3 = positions.reshape(B, 1, S)
    inv_freq = ROPE_THETA ** (-jnp.arange(0, HEAD_DIM, 2, dtype=F32) / HEAD_DIM)
    invf = jnp.broadcast_to(inv_freq.reshape(-1, 1), (HEAD_DIM // 2, LANES))
    for l in range(depth):
        wfm, wtok, gains, wa, waT, ba, baT = _layer_weights(
            w_in[l], swa_q_norm[l], swa_k_norm[l], moba_q_norm[l], moba_k_norm[l],
            gla_w_alpha[l], gla_b_alpha[l])
        tok, fm = _in_proj(x, pos3, invf, attn_norm[l].reshape(1, D), wtok, wfm, gains)
        sinks_b = jnp.broadcast_to(swa_sinks[l].reshape(-1, 1), (SWA_Q_HEADS, SWA_TQ))
        yaT = _swa(fm, sinks_b)
        ybT = _moba(fm)
        yc = _gla(tok, fm, wa, waT, ba, baT, gla_out_norm[l].reshape(1, GLA_DV))
        x = _merge(x, yaT, ybT, yc, tok, w_branch_swa[l].astype(BF16), w_branch_moba[l].astype(BF16),
                   w_branch_gla[l].astype(BF16), w_out[l].astype(BF16))
        x = _ffn(x, ffn_norm[l].reshape(1, D), w_ffn_gate[l].astype(BF16), w_ffn_up[l].astype(BF16),
                 w_ffn_down[l].astype(BF16))
    return x
```

```python
import functools

import jax
import jax.numpy as jnp
from jax import lax
from jax.experimental import pallas as pl
from jax.experimental.pallas import tpu as pltpu

F32 = jnp.float32
BF16 = jnp.bfloat16

D_MODEL = 1024
HEAD_DIM = 64
ROPE_THETA = 10000.0
EPS = 1e-6
SWA_Q_HEADS = 8
SWA_KV_HEADS = 2
SWA_WINDOW = 128
MOBA_HEADS = 8
MOBA_BLOCK = 256
MOBA_TOPK = 3
GLA_HEADS = 4
GLA_DK = 64
GLA_DV = 128
GLA_GATE_RANK = 16
GLA_TAU = 16.0
GLA_CHUNK = 64
N_BRANCH = 3
D_FF = 2816
QK_SCALE = HEAD_DIM ** -0.5
NEG = -1e30

VMEM_LIMIT_BYTES = 56 * 1024 * 1024
LANES = 128

_C_QA, _C_KA, _C_VA = (0, 512), (512, 640), (640, 768)
_C_QB, _C_KB, _C_VB = (768, 1280), (1280, 1792), (1792, 2304)
_C_QG, _C_KG, _C_VG = (2304, 2560), (2560, 2816), (2816, 3328)
_C_AL, _C_RG, _C_GATES = (3328, 3344), (3344, 3856), (3856, 6928)

FM_QB, FM_KB, FM_VB, FM_QA, FM_KA, FM_VA, FM_KG, FM_AL = 0, 512, 1024, 1536, 2048, 2176, 2304, 2560
FM_ROWS = 2576
TK_GATES, TK_VG, TK_RG, TK_QG, TK_AL = 0, 3072, 3584, 4096, 4352
TK_COLS = 4480

IN_TM = 512
MERGE_TM = 512
FFN_TM = 512
FFN_CHUNK = 256
SWA_TQ = 128
GLA_STEP = 256


def _sigmoid(x):
    return 1.0 / (1.0 + jnp.exp(-x))


def _log_sigmoid(x):
    return jnp.minimum(x, 0.0) - jnp.log(1.0 + jnp.exp(-jnp.abs(x)))


def _iota_div(shape, dim, n):
    assert n & (n - 1) == 0
    return lax.shift_right_logical(lax.broadcasted_iota(jnp.int32, shape, dim), n.bit_length() - 1)


def _fold_rows(x, op):
    n = x.shape[0]
    if n > 32 and n % 32 == 0:
        acc = x[0:32]
        for g in range(1, n // 32):
            acc = op(acc, x[g * 32:(g + 1) * 32])
        x, n = acc, 32
    while n > 8 and n % 16 == 0:
        n //= 2
        x = op(x[:n], x[n:])
    return x


def _colmax(x):
    return jnp.max(_fold_rows(x, jnp.maximum), axis=0, keepdims=True)


def _colsum(x):
    return jnp.sum(_fold_rows(x, jnp.add), axis=0, keepdims=True)


def _split_bf16(x):
    hi = x.astype(BF16)
    lo = (x - hi.astype(F32)).astype(BF16)
    return hi, lo


def _cparams(sem):
    return pltpu.CompilerParams(dimension_semantics=sem, vmem_limit_bytes=VMEM_LIMIT_BYTES)


def _in_proj_kernel(x_ref, pos_ref, invf_ref, g_ref, wtok_ref, wfm_ref, gains_ref, tok_ref, fm_ref, h_scr):
    x = x_ref[0]
    h = x * lax.rsqrt(jnp.mean(x * x, axis=-1, keepdims=True) + EPS) * g_ref[...]
    h_scr[...] = h.astype(BF16)

    step = 512
    for c0 in range(0, TK_COLS, step):
        c1 = min(c0 + step, TK_COLS)
        tok_ref[0, :, c0:c1] = jnp.dot(h_scr[...], wtok_ref[:, c0:c1],
                                       preferred_element_type=F32).astype(BF16)

    ang = invf_ref[:, 0:1] * pos_ref[0].astype(F32)
    cos, sin = jnp.cos(ang), jnp.sin(ang)

    def fm_block(r0, nrows):
        return lax.dot_general(wfm_ref[r0:r0 + nrows, :], h_scr[...], (((1,), (1,)), ((), ())),
                               preferred_element_type=F32)

    def norm_rope(a, gain_col, scale):
        y = a * lax.rsqrt(jnp.mean(a * a, axis=0, keepdims=True) + EPS) * gains_ref[:, gain_col:gain_col + 1]
        y1, y2 = y[:HEAD_DIM // 2], y[HEAD_DIM // 2:]
        out = jnp.concatenate([y1 * cos - y2 * sin, y2 * cos + y1 * sin], axis=0)
        if scale != 1.0:
            out = out * scale
        return out

    def finish(r0, n, gcol, scale, a):
        if gcol is None:
            fm_ref[0, r0:r0 + n, :] = a.astype(BF16)
            return
        for hh in range(n // HEAD_DIM):
            fm_ref[0, r0 + hh * HEAD_DIM:r0 + (hh + 1) * HEAD_DIM, :] = norm_rope(
                a[hh * HEAD_DIM:(hh + 1) * HEAD_DIM], gcol, scale).astype(BF16)

    blk = 256
    regions = (
        (FM_QB, 512, 0, QK_SCALE), (FM_VB, 512, None, 1.0), (FM_KB, 512, 1, 1.0), (FM_KG, 256, None, 1.0),
        (FM_QA, 512, 2, QK_SCALE), (FM_VA, 128, None, 1.0), (FM_KA, 128, 3, 1.0), (FM_AL, 16, None, 1.0))
    blocks = [(r0, min(blk, row0 + rows - r0), gcol, scale)
              for row0, rows, gcol, scale in regions for r0 in range(row0, row0 + rows, blk)]
    a_next = fm_block(*blocks[0][:2])
    for n, block in enumerate(blocks):
        a_cur = a_next
        if n + 1 < len(blocks):
            a_next = fm_block(*blocks[n + 1][:2])
        finish(*block, a_cur)


def _in_proj(x, pos3, invf, g, wtok, wfm, gains):
    B, S, D = x.shape
    tm = IN_TM
    const = lambda shape: pl.BlockSpec(shape, lambda b, i: (0,) * len(shape))
    return pl.pallas_call(
        _in_proj_kernel,
        grid=(B, S // tm),
        in_specs=[
            pl.BlockSpec((1, tm, D), lambda b, i: (b, i, 0)),
            pl.BlockSpec((1, 1, tm), lambda b, i: (b, 0, i)),
            const((HEAD_DIM // 2, LANES)),
            const((1, D)),
            const((D, TK_COLS)),
            const((FM_ROWS, D)),
            const((HEAD_DIM, LANES)),
        ],
        out_specs=[
            pl.BlockSpec((1, tm, TK_COLS), lambda b, i: (b, i, 0)),
            pl.BlockSpec((1, FM_ROWS, tm), lambda b, i: (b, 0, i)),
        ],
        out_shape=[
            jax.ShapeDtypeStruct((B, S, TK_COLS), BF16),
            jax.ShapeDtypeStruct((B, FM_ROWS, S), BF16),
        ],
        scratch_shapes=[pltpu.VMEM((tm, D), BF16)],
        compiler_params=_cparams(("parallel", "parallel")),
        name="in_proj",
    )(x, pos3, invf, g, wtok, wfm, gains)


def _swa_kernel(q_ref, k_ref, v_ref, sink_ref, o_ref, k_scr):
    tq = SWA_TQ
    assert tq == SWA_WINDOW
    S = k_ref.shape[2]
    nt = S // tq
    grp = SWA_Q_HEADS // SWA_KV_HEADS

    head_row = _iota_div((LANES, LANES), 0, HEAD_DIM)
    for j in range(S // LANES):
        kt = k_ref[0, :, j * LANES:(j + 1) * LANES].astype(F32)
        for g in range(SWA_KV_HEADS):
            k_scr[g, j * LANES:(j + 1) * LANES, :] = jnp.where(head_row == g, kt, 0.0).T.astype(BF16)

    r = lax.broadcasted_iota(jnp.int32, (2 * tq, tq), 0)
    c = lax.broadcasted_iota(jnp.int32, (2 * tq, tq), 1)
    bias_two = jnp.where((r > c) & (r <= c + tq), 0.0, NEG).astype(F32)
    bias_first = jnp.where(r <= c, 0.0, NEG).astype(F32)

    def scores(g, i):
        k0 = max(i - 1, 0) * tq
        qg = jnp.concatenate(
            [q_ref[0, (g * grp + hh) * HEAD_DIM:(g * grp + hh + 1) * HEAD_DIM, i * tq:(i + 1) * tq]
             for hh in range(grp)], axis=1)
        z = jnp.concatenate([qg, qg], axis=0)
        return jnp.dot(k_scr[g, k0:k0 + 2 * tq, :], z, preferred_element_type=F32)

    def attend(g, i, s):
        cols = slice(i * tq, (i + 1) * tq)
        k0 = max(i - 1, 0) * tq
        bias = bias_two if i > 0 else bias_first
        ps, invs = [], []
        for hh in range(grp):
            h = g * grp + hh
            sh = s[:, hh * tq:(hh + 1) * tq] + bias
            sink = sink_ref[h:h + 1, :]
            m = jnp.maximum(_colmax(sh), sink)
            p = jnp.exp(sh - m)
            den = _colsum(p) + jnp.exp(sink - m)
            ps.append(p.astype(BF16))
            invs.append(1.0 / den)
        pcat = jnp.concatenate(ps, axis=1)
        o = jnp.dot(v_ref[0, g * HEAD_DIM:(g + 1) * HEAD_DIM, k0:k0 + 2 * tq], pcat,
                    preferred_element_type=F32)
        for hh in range(grp):
            h = g * grp + hh
            o_ref[0, h * HEAD_DIM:(h + 1) * HEAD_DIM, cols] = (
                o[:, hh * tq:(hh + 1) * tq] * invs[hh]).astype(BF16)

    units = [(g, i) for i in range(nt) for g in range(SWA_KV_HEADS)]
    s_next = scores(*units[0])
    for n, unit in enumerate(units):
        s_cur = s_next
        if n + 1 < len(units):
            s_next = scores(*units[n + 1])
        attend(*unit, s_cur)


def _swa(fm, sinks_b):
    B, _, S = fm.shape
    return pl.pallas_call(
        _swa_kernel,
        grid=(B,),
        in_specs=[
            pl.BlockSpec((1, 512, S), lambda b: (b, FM_QA // 512, 0)),
            pl.BlockSpec((1, 128, S), lambda b: (b, FM_KA // 128, 0)),
            pl.BlockSpec((1, 128, S), lambda b: (b, FM_VA // 128, 0)),
            pl.BlockSpec((SWA_Q_HEADS, SWA_TQ), lambda b: (0, 0)),
        ],
        out_specs=pl.BlockSpec((1, 512, S), lambda b: (b, 0, 0)),
        out_shape=jax.ShapeDtypeStruct((B, 512, S), BF16),
        scratch_shapes=[pltpu.VMEM((SWA_KV_HEADS, S, LANES), BF16)],
        compiler_params=_cparams(("parallel",)),
        name="swa",
    )(fm, fm, fm, sinks_b)


def _moba_kernel(q_ref, k_ref, v_ref, o_ref, k_scr):
    L = MOBA_BLOCK
    S = k_ref.shape[2]
    nb = S // L
    assert nb == 8
    nbp = 16

    head_row = _iota_div((LANES, LANES), 0, HEAD_DIM)
    for j in range(S // LANES):
        kt = k_ref[0, :, j * LANES:(j + 1) * LANES].astype(F32)
        for hh in range(2):
            k_scr[hh, j * LANES:(j + 1) * LANES, :] = jnp.where(head_row == hh, kt, 0.0).T.astype(BF16)

    sel = jnp.where(_iota_div((nbp, S), 1, L) == lax.broadcasted_iota(jnp.int32, (nbp, S), 0),
                    1.0 / L, 0.0).astype(BF16)
    q = q_ref[0]
    jrow = lax.broadcasted_iota(jnp.int32, (nb, S), 0)
    qblk = _iota_div((nb, S), 1, L)
    causal = (lax.broadcasted_iota(jnp.int32, (L, L), 0) <= lax.broadcasted_iota(jnp.int32, (L, L), 1))

    biases = []
    for hh in range(2):
        kmean = jnp.dot(sel, k_scr[hh], preferred_element_type=F32)
        km_hi, km_lo = _split_bf16(kmean)
        gate = (jnp.dot(km_hi, q, preferred_element_type=F32)
                + jnp.dot(km_lo, q, preferred_element_type=F32))[:nb]
        rank = jnp.zeros((nb, S), jnp.int32)
        for d in range(1, nb):
            other = pltpu.roll(gate, d, axis=0)
            wrapped = jrow < d
            j2 = jnp.where(wrapped, jrow - d + nb, jrow - d)
            ahead = (other > gate) | ((other == gate) & jnp.logical_not(wrapped))
            rank = rank + jnp.where(ahead & (qblk > j2), 1, 0)
        biases.append(jnp.where((rank < MOBA_TOPK) & (qblk > jrow), 0.0, NEG).astype(F32))

    def scores(hh, i):
        return jnp.dot(k_scr[hh, 0:(i + 1) * L, :], q[:, i * L:(i + 1) * L], preferred_element_type=F32)

    def attend(hh, i, s_all):
        cols = slice(i * L, (i + 1) * L)
        s_own = jnp.where(causal, s_all[i * L:(i + 1) * L], NEG)
        m = _colmax(s_own)
        past = []
        for j in range(i):
            s_j = s_all[j * L:(j + 1) * L]
            b_j = biases[hh][j:j + 1, cols]
            m = jnp.maximum(m, _colmax(s_j) + b_j)
            past.append((s_j, b_j))
        vT = v_ref.at[0, hh * HEAD_DIM:(hh + 1) * HEAD_DIM, :]
        p = jnp.exp(s_own - m)
        l = _colsum(p)
        acc = jnp.dot(vT[:, cols], p.astype(BF16), preferred_element_type=F32)
        for j, (s_j, b_j) in enumerate(past):
            p = jnp.exp(s_j + (b_j - m))
            l = l + _colsum(p)
            acc = acc + jnp.dot(vT[:, j * L:(j + 1) * L], p.astype(BF16), preferred_element_type=F32)
        o_ref[0, hh * HEAD_DIM:(hh + 1) * HEAD_DIM, cols] = (acc * (1.0 / l)).astype(BF16)

    units = [(hh, i) for i in range(nb) for hh in range(2)]
    s_next = scores(*units[0])
    for n, unit in enumerate(units):
        s_cur = s_next
        if n + 1 < len(units):
            s_next = scores(*units[n + 1])
        attend(*unit, s_cur)


def _moba(fm):
    B, _, S = fm.shape
    hp = MOBA_HEADS // 2
    return pl.pallas_call(
        _moba_kernel,
        grid=(B, hp),
        in_specs=[
            pl.BlockSpec((1, 128, S), lambda b, p: (b, FM_QB // 128 + p, 0)),
            pl.BlockSpec((1, 128, S), lambda b, p: (b, FM_KB // 128 + p, 0)),
            pl.BlockSpec((1, 128, S), lambda b, p: (b, FM_VB // 128 + p, 0)),
        ],
        out_specs=pl.BlockSpec((1, 128, S), lambda b, p: (b, p, 0)),
        out_shape=jax.ShapeDtypeStruct((B, 512, S), BF16),
        scratch_shapes=[pltpu.VMEM((2, S, LANES), BF16)],
        compiler_params=_cparams(("parallel", "parallel")),
        name="moba",
    )(fm, fm, fm)


def _gla_kernel(q_ref, kT_ref, v_ref, al_ref, alT_ref, r_ref, wa_ref, waT_ref, ba_ref, baT_ref, gn_ref,
                o_ref, state_scr):
    T = q_ref.shape[1]
    C = GLA_CHUNK
    nc = T // C
    H = GLA_HEADS
    W = H * GLA_DK

    @pl.when(pl.program_id(1) == 0)
    def _():
        state_scr[...] = jnp.zeros_like(state_scr)

    z = jnp.dot(al_ref[0], wa_ref[...], preferred_element_type=F32) + ba_ref[...]
    la = _log_sigmoid(z) * (1.0 / GLA_TAU)
    zT = jnp.dot(waT_ref[...], alT_ref[0], preferred_element_type=F32) + baT_ref[:, 0:1]
    laT = _log_sigmoid(zT) * (1.0 / GLA_TAU)

    ti = lax.broadcasted_iota(jnp.int32, (T, T), 0)
    tj = lax.broadcasted_iota(jnp.int32, (T, T), 1)
    same = _iota_div((T, T), 0, C) == _iota_div((T, T), 1, C)
    tril_bd = same & (tj <= ti)
    low = jnp.where(tril_bd, 1.0, 0.0).astype(BF16)
    upp = jnp.where(same & (ti <= tj), 1.0, 0.0).astype(BF16)
    ones_bd = jnp.where(same, 1.0, 0.0).astype(BF16)
    selc = jnp.where(_iota_div((T, nc * GLA_DV), 0, C) == _iota_div((T, nc * GLA_DV), 1, GLA_DV),
                     1.0, 0.0).astype(BF16)

    la_hi, la_lo = _split_bf16(la)
    b = jnp.dot(low, la_hi, preferred_element_type=F32) + jnp.dot(low, la_lo, preferred_element_type=F32)
    laT_hi, laT_lo = _split_bf16(laT)
    two = lambda rhs: (jnp.dot(laT_hi, rhs, preferred_element_type=F32)
                       + jnp.dot(laT_lo, rhs, preferred_element_type=F32))
    bT = two(upp)
    btotT = two(ones_bd)
    decay = jnp.exp(two(selc))

    qd = (q_ref[0].astype(F32) * QK_SCALE * jnp.exp(b)).astype(BF16)
    kT = kT_ref[0].astype(F32)
    kTd = (kT * jnp.exp(-bT)).astype(BF16)
    kTt = (kT * jnp.exp(btotT - bT)).astype(BF16)

    lane_in_pair = _iota_div((T, 2 * GLA_DK), 1, GLA_DK)
    blkmask = _iota_div((nc * GLA_DK, T), 0, GLA_DK) == _iota_div((nc * GLA_DK, T), 1, C)
    gn = gn_ref[...]

    qms, states = [], []
    for h in range(H):
        p = h // 2
        qp = qd[:, p * 128:(p + 1) * 128]
        qms.append(jnp.where(lane_in_pair == (h % 2), qp, jnp.zeros_like(qp)))
        v_h = v_ref[0, :, h * GLA_DV:(h + 1) * GLA_DV]
        kt_h = kTt[h * GLA_DK:(h + 1) * GLA_DK, :]
        kst = jnp.where(blkmask, jnp.concatenate([kt_h] * nc, axis=0), jnp.zeros((nc * GLA_DK, T), BF16))
        ds_all = jnp.dot(kst, v_h, preferred_element_type=F32)
        st = state_scr[h * GLA_DK:(h + 1) * GLA_DK, :]
        before = []
        for c in range(nc):
            before.append(st.astype(BF16))
            st = (decay[h * GLA_DK:(h + 1) * GLA_DK, c * GLA_DV:(c + 1) * GLA_DV] * st
                  + ds_all[c * GLA_DK:(c + 1) * GLA_DK])
        state_scr[h * GLA_DK:(h + 1) * GLA_DK, :] = st
        states.append(before)

    for h in range(H):
        p = h // 2
        v_h = v_ref[0, :, h * GLA_DV:(h + 1) * GLA_DV]
        a = jnp.dot(qms[h], kTd[p * 128:(p + 1) * 128, :], preferred_element_type=F32)
        a = jnp.where(tril_bd, a, 0.0).astype(BF16)
        o = jnp.dot(a, v_h, preferred_element_type=F32)
        rhs = jnp.concatenate(
            [jnp.concatenate([states[2 * p][c], states[2 * p + 1][c]], axis=0) for c in range(nc)], axis=1)
        oi = jnp.dot(qms[h], rhs, preferred_element_type=F32)
        o = o + jnp.concatenate(
            [oi[c * C:(c + 1) * C, c * GLA_DV:(c + 1) * GLA_DV] for c in range(nc)], axis=0)
        y = o * lax.rsqrt(jnp.mean(o * o, axis=-1, keepdims=True) + EPS) * gn
        r = r_ref[0, :, h * GLA_DV:(h + 1) * GLA_DV].astype(F32)
        o_ref[0, :, h * GLA_DV:(h + 1) * GLA_DV] = (y * (r * _sigmoid(r))).astype(BF16)


def _gla(tok, fm, wa, waT, ba, baT, gn):
    B, S, _ = tok.shape
    T = GLA_STEP
    const = lambda shape: pl.BlockSpec(shape, lambda b, s: (0,) * len(shape))
    return pl.pallas_call(
        _gla_kernel,
        grid=(B, S // T),
        in_specs=[
            pl.BlockSpec((1, T, 256), lambda b, s: (b, s, TK_QG // 256)),
            pl.BlockSpec((1, 256, T), lambda b, s: (b, FM_KG // 256, s)),
            pl.BlockSpec((1, T, 512), lambda b, s: (b, s, TK_VG // 512)),
            pl.BlockSpec((1, T, 128), lambda b, s: (b, s, TK_AL // 128)),
            pl.BlockSpec((1, 16, T), lambda b, s: (b, FM_AL // 16, s)),
            pl.BlockSpec((1, T, 512), lambda b, s: (b, s, TK_RG // 512)),
            const((128, 256)), const((256, 16)), const((1, 256)), const((256, LANES)), const((1, GLA_DV)),
        ],
        out_specs=pl.BlockSpec((1, T, 512), lambda b, s: (b, s, 0)),
        out_shape=jax.ShapeDtypeStruct((B, S, 512), BF16),
        scratch_shapes=[pltpu.VMEM((GLA_HEADS * GLA_DK, GLA_DV), F32)],
        compiler_params=_cparams(("parallel", "arbitrary")),
        name="gla",
    )(tok, fm, tok, tok, fm, tok, wa, waT, ba, baT, gn)


def _merge_kernel(x_ref, yaT_ref, ybT_ref, yc_ref, g_ref, ws_ref, wm_ref, wg_ref, wo_ref, o_ref):
    tn = (((0,), (0,)), ((), ()))
    pa = lax.dot_general(yaT_ref[0], ws_ref[...], tn, preferred_element_type=F32)
    pb = lax.dot_general(ybT_ref[0], wm_ref[...], tn, preferred_element_type=F32)
    pc = jnp.dot(yc_ref[0], wg_ref[...], preferred_element_type=F32)
    D = D_MODEL
    merged = (_sigmoid(g_ref[0, :, 0:D].astype(F32)) * pa
              + _sigmoid(g_ref[0, :, D:2 * D].astype(F32)) * pb
              + _sigmoid(g_ref[0, :, 2 * D:3 * D].astype(F32)) * pc)
    o_ref[0] = x_ref[0] + jnp.dot(merged.astype(BF16), wo_ref[...], preferred_element_type=F32)


def _merge(x, yaT, ybT, yc, tok, ws, wm, wg, wo):
    B, S, D = x.shape
    tm = MERGE_TM
    const = lambda shape: pl.BlockSpec(shape, lambda b, i: (0,) * len(shape))
    return pl.pallas_call(
        _merge_kernel,
        grid=(B, S // tm),
        in_specs=[
            pl.BlockSpec((1, tm, D), lambda b, i: (b, i, 0)),
            pl.BlockSpec((1, 512, tm), lambda b, i: (b, 0, i)),
            pl.BlockSpec((1, 512, tm), lambda b, i: (b, 0, i)),
            pl.BlockSpec((1, tm, 512), lambda b, i: (b, i, 0)),
            pl.BlockSpec((1, tm, N_BRANCH * D), lambda b, i: (b, i, TK_GATES)),
            const((512, D)), const((512, D)), const((512, D)), const((D, D)),
        ],
        out_specs=pl.BlockSpec((1, tm, D), lambda b, i: (b, i, 0)),
        out_shape=jax.ShapeDtypeStruct((B, S, D), F32),
        compiler_params=_cparams(("parallel", "parallel")),
        name="merge",
    )(x, yaT, ybT, yc, tok, ws, wm, wg, wo)


def _ffn_kernel(x_ref, g_ref, wg_ref, wu_ref, wd_ref, o_ref, h_scr):
    x = x_ref[0]
    h = x * lax.rsqrt(jnp.mean(x * x, axis=-1, keepdims=True) + EPS) * g_ref[...]
    h_scr[...] = h.astype(BF16)
    acc = x
    for c0 in range(0, D_FF, FFN_CHUNK):
        c1 = c0 + FFN_CHUNK
        gt = jnp.dot(h_scr[...], wg_ref[:, c0:c1], preferred_element_type=F32)
        up = jnp.dot(h_scr[...], wu_ref[:, c0:c1], preferred_element_type=F32)
        a = (gt * _sigmoid(gt) * up).astype(BF16)
        acc = acc + jnp.dot(a, wd_ref[c0:c1, :], preferred_element_type=F32)
    o_ref[0] = acc


def _ffn(x, g, wg, wu, wd):
    B, S, D = x.shape
    tm = FFN_TM
    const = lambda shape: pl.BlockSpec(shape, lambda b, i: (0,) * len(shape))
    return pl.pallas_call(
        _ffn_kernel,
        grid=(B, S // tm),
        in_specs=[
            pl.BlockSpec((1, tm, D), lambda b, i: (b, i, 0)),
            const((1, D)), const((D, D_FF)), const((D, D_FF)), const((D_FF, D)),
        ],
        out_specs=pl.BlockSpec((1, tm, D), lambda b, i: (b, i, 0)),
        out_shape=jax.ShapeDtypeStruct((B, S, D), F32),
        scratch_shapes=[pltpu.VMEM((tm, D), BF16)],
        compiler_params=_cparams(("parallel", "parallel")),
        name="ffn",
    )(x, g, wg, wu, wd)


def _cols(w, *ranges):
    return jnp.concatenate([w[:, a:b] for a, b in ranges], axis=1)


def _layer_weights(w_in, swa_qn, swa_kn, moba_qn, moba_kn, w_alpha, b_alpha):
    wfm = _cols(w_in, _C_QB, _C_KB, _C_VB, _C_QA, _C_KA, _C_VA, _C_KG, _C_AL).T.astype(BF16)
    al_pad = jnp.pad(w_in[:, _C_AL[0]:_C_AL[1]], ((0, 0), (0, LANES - GLA_GATE_RANK)))
    wtok = jnp.concatenate([_cols(w_in, _C_GATES, _C_VG, _C_RG, _C_QG), al_pad], axis=1).astype(BF16)
    gains = jnp.pad(jnp.stack([moba_qn, moba_kn, swa_qn, swa_kn], axis=1), ((0, 0), (0, LANES - 4)))
    wa = jnp.pad(w_alpha, ((0, LANES - GLA_GATE_RANK), (0, 0))).astype(BF16)
    waT = w_alpha.T.astype(BF16)
    ba = b_alpha.reshape(1, -1)
    baT = jnp.broadcast_to(b_alpha.reshape(-1, 1), (b_alpha.shape[0], LANES))
    return wfm, wtok, gains, wa, waT, ba, baT


def kernel(x, positions, attn_norm, w_in, swa_q_norm, swa_k_norm, swa_sinks, moba_q_norm, moba_k_norm,
           gla_w_alpha, gla_b_alpha, gla_out_norm, w_branch_swa, w_branch_moba, w_branch_gla, w_out,
           ffn_norm, w_ffn_gate, w_ffn_up, w_ffn_down):
    B, S, D = x.shape
    depth = w_in.shape[0]
    pos3 = positions.reshape(B, 1, S)
    inv_freq = ROPE_THETA ** (-jnp.arange(0, HEAD_DIM, 2, dtype=F32) / HEAD_DIM)
    invf = jnp.broadcast_to(inv_freq.reshape(-1, 1), (HEAD_DIM // 2, LANES))
    for l in range(depth):
        wfm, wtok, gains, wa, waT, ba, baT = _layer_weights(
            w_in[l], swa_q_norm[l], swa_k_norm[l], moba_q_norm[l], moba_k_norm[l],
            gla_w_alpha[l], gla_b_alpha[l])
        tok, fm = _in_proj(x, pos3, invf, attn_norm[l].reshape(1, D), wtok, wfm, gains)
        sinks_b = jnp.broadcast_to(swa_sinks[l].reshape(-1, 1), (SWA_Q_HEADS, SWA_TQ))
        yaT = _swa(fm, sinks_b)
        ybT = _moba(fm)
        yc = _gla(tok, fm, wa, waT, ba, baT, gla_out_norm[l].reshape(1, GLA_DV))
        x = _merge(x, yaT, ybT, yc, tok, w_branch_swa[l].astype(BF16), w_branch_moba[l].astype(BF16),
                   w_branch_gla[l].astype(BF16), w_out[l].astype(BF16))
        x = _ffn(x, ffn_norm[l].reshape(1, D), w_ffn_gate[l].astype(BF16), w_ffn_up[l].astype(BF16),
                 w_ffn_down[l].astype(BF16))
    return x
```

```python
import functools

import jax
import jax.numpy as jnp
from jax import lax
from jax.experimental import pallas as pl
from jax.experimental.pallas import tpu as pltpu

F32 = jnp.float32
BF16 = jnp.bfloat16

D_MODEL = 1024
HEAD_DIM = 64
ROPE_THETA = 10000.0
EPS = 1e-6
SWA_Q_HEADS = 8
SWA_KV_HEADS = 2
SWA_WINDOW = 128
MOBA_HEADS = 8
MOBA_BLOCK = 256
MOBA_TOPK = 3
GLA_HEADS = 4
GLA_DK = 64
GLA_DV = 128
GLA_GATE_RANK = 16
GLA_TAU = 16.0
GLA_CHUNK = 64
N_BRANCH = 3
D_FF = 2816
QK_SCALE = HEAD_DIM ** -0.5
NEG = -1e30

VMEM_LIMIT_BYTES = 56 * 1024 * 1024
LANES = 128

_C_QA, _C_KA, _C_VA = (0, 512), (512, 640), (640, 768)
_C_QB, _C_KB, _C_VB = (768, 1280), (1280, 1792), (1792, 2304)
_C_QG, _C_KG, _C_VG = (2304, 2560), (2560, 2816), (2816, 3328)
_C_AL, _C_RG, _C_GATES = (3328, 3344), (3344, 3856), (3856, 6928)

FM_QB, FM_KB, FM_VB, FM_QA, FM_KA, FM_VA, FM_KG, FM_AL = 0, 512, 1024, 1536, 2048, 2176, 2304, 2560
FM_ROWS = 2576
TK_GATES, TK_VG, TK_RG, TK_QG, TK_AL = 0, 3072, 3584, 4096, 4352
TK_COLS = 4480

IN_TM = 512
MERGE_TM = 512
FFN_TM = 512
FFN_CHUNK = 256
SWA_TQ = 128
GLA_STEP = 256
GLA_BATCH_ROWS = 4


def _sigmoid(x):
    return 1.0 / (1.0 + jnp.exp(-x))


def _log_sigmoid(x):
    return jnp.minimum(x, 0.0) - jnp.log(1.0 + jnp.exp(-jnp.abs(x)))


def _iota_div(shape, dim, n):
    assert n & (n - 1) == 0
    return lax.shift_right_logical(lax.broadcasted_iota(jnp.int32, shape, dim), n.bit_length() - 1)


def _fold_rows(x, op):
    n = x.shape[0]
    if n > 32 and n % 32 == 0:
        acc = x[0:32]
        for g in range(1, n // 32):
            acc = op(acc, x[g * 32:(g + 1) * 32])
        x, n = acc, 32
    while n > 8 and n % 16 == 0:
        n //= 2
        x = op(x[:n], x[n:])
    return x


def _colmax(x):
    return jnp.max(_fold_rows(x, jnp.maximum), axis=0, keepdims=True)


def _colsum(x):
    return jnp.sum(_fold_rows(x, jnp.add), axis=0, keepdims=True)


def _emit_pipelined(units, produce, consume, lookahead=2):
    pending = [produce(*u) for u in units[:lookahead]]
    for n, unit in enumerate(units):
        cur = pending.pop(0)
        if n + lookahead < len(units):
            pending.append(produce(*units[n + lookahead]))
        consume(*unit, cur)


def _split_bf16(x):
    hi = x.astype(BF16)
    lo = (x - hi.astype(F32)).astype(BF16)
    return hi, lo


def _layer_spec(shape, layer):
    zeros = (0,) * len(shape)
    return pl.BlockSpec((None,) + tuple(shape), lambda *_: (layer,) + zeros)


def _cparams(sem):
    return pltpu.CompilerParams(dimension_semantics=sem, vmem_limit_bytes=VMEM_LIMIT_BYTES)


def _rope_kernel(pos_ref, invf_ref, o_ref):
    ang = invf_ref[:, 0:1] * pos_ref[0].astype(F32)
    o_ref[0, 0:HEAD_DIM // 2, :] = jnp.cos(ang)
    o_ref[0, HEAD_DIM // 2:, :] = jnp.sin(ang)


def _rope_tables(pos3, invf):
    B, _, S = pos3.shape
    return pl.pallas_call(
        _rope_kernel,
        grid=(B,),
        in_specs=[pl.BlockSpec((1, 1, S), lambda b: (b, 0, 0)),
                  pl.BlockSpec((HEAD_DIM // 2, LANES), lambda b: (0, 0))],
        out_specs=pl.BlockSpec((1, HEAD_DIM, S), lambda b: (b, 0, 0)),
        out_shape=jax.ShapeDtypeStruct((B, HEAD_DIM, S), F32),
        compiler_params=_cparams(("parallel",)),
        name="rope_tables",
    )(pos3, invf)


def _in_proj_kernel(x_ref, rope_ref, g_ref, wtok_ref, wfm_ref, gains_ref, tok_ref, fm_ref, h_scr):
    x = x_ref[0]
    h = x * lax.rsqrt(jnp.mean(x * x, axis=-1, keepdims=True) + EPS) * g_ref[...]
    h_scr[...] = h.astype(BF16)

    def tok_chunk(c0, c1):
        tok_ref[0, :, c0:c1] = jnp.dot(h_scr[...], wtok_ref[:, c0:c1], preferred_element_type=F32).astype(BF16)

    def fm_block(r0, nrows):
        return lax.dot_general(wfm_ref[r0:r0 + nrows, :], h_scr[...], (((1,), (1,)), ((), ())),
                               preferred_element_type=F32)

    step = 512
    tok_chunks = [(c0, min(c0 + step, TK_COLS)) for c0 in range(0, TK_COLS, step)]
    tok_chunk(*tok_chunks.pop(0))

    cos = rope_ref[0, 0:HEAD_DIM // 2, :]
    sin = rope_ref[0, HEAD_DIM // 2:, :]

    def norm_rope(a, gain_col):
        y = a * lax.rsqrt(jnp.mean(a * a, axis=0, keepdims=True) + EPS) * gains_ref[:, gain_col:gain_col + 1]
        y1, y2 = y[:HEAD_DIM // 2], y[HEAD_DIM // 2:]
        return jnp.concatenate([y1 * cos - y2 * sin, y2 * cos + y1 * sin], axis=0)

    blk = 256
    split = lambda row0, rows: [(r0, min(blk, row0 + rows - r0)) for r0 in range(row0, row0 + rows, blk)]
    for row0, rows, gcol in ((FM_QB, 512, 0), (FM_KB, 512, 1), (FM_QA, 512, 2), (FM_KA, 128, 3)):
        for r0, n in split(row0, rows):
            a = fm_block(r0, n)
            if tok_chunks:
                tok_chunk(*tok_chunks.pop(0))
            for hh in range(n // HEAD_DIM):
                fm_ref[0, r0 + hh * HEAD_DIM:r0 + (hh + 1) * HEAD_DIM, :] = norm_rope(
                    a[hh * HEAD_DIM:(hh + 1) * HEAD_DIM], gcol).astype(BF16)
    for c in tok_chunks:
        tok_chunk(*c)
    for r0, n in ((FM_VB, 512), (FM_VA, FM_ROWS - FM_VA)):
        fm_ref[0, r0:r0 + n, :] = fm_block(r0, n).astype(BF16)


def _in_proj(x, rope, layer, g, wtok, wfm, gains):
    B, S, D = x.shape
    tm = IN_TM
    const = lambda shape: _layer_spec(shape, layer)
    return pl.pallas_call(
        _in_proj_kernel,
        grid=(B, S // tm),
        in_specs=[
            pl.BlockSpec((1, tm, D), lambda b, i: (b, i, 0)),
            pl.BlockSpec((1, HEAD_DIM, tm), lambda b, i: (b, 0, i)),
            const((1, D)),
            const((D, TK_COLS)),
            const((FM_ROWS, D)),
            const((HEAD_DIM, LANES)),
        ],
        out_specs=[
            pl.BlockSpec((1, tm, TK_COLS), lambda b, i: (b, i, 0)),
            pl.BlockSpec((1, FM_ROWS, tm), lambda b, i: (b, 0, i)),
        ],
        out_shape=[
            jax.ShapeDtypeStruct((B, S, TK_COLS), BF16),
            jax.ShapeDtypeStruct((B, FM_ROWS, S), BF16),
        ],
        scratch_shapes=[pltpu.VMEM((tm, D), BF16)],
        compiler_params=_cparams(("parallel", "parallel")),
        name="in_proj",
    )(x, rope, g, wtok, wfm, gains)


def _swa_kernel(q_ref, k_ref, v_ref, sink_ref, o_ref, k_scr):
    tq = SWA_TQ
    assert tq == SWA_WINDOW
    S = k_ref.shape[2]
    nt = S // tq
    grp = SWA_Q_HEADS // SWA_KV_HEADS

    head_row = _iota_div((LANES, LANES), 0, HEAD_DIM)
    for j in range(S // LANES):
        kt = k_ref[0, :, j * LANES:(j + 1) * LANES].astype(F32)
        for g in range(SWA_KV_HEADS):
            k_scr[g, j * LANES:(j + 1) * LANES, :] = jnp.where(head_row == g, kt, 0.0).T.astype(BF16)

    r = lax.broadcasted_iota(jnp.int32, (2 * tq, tq), 0)
    c = lax.broadcasted_iota(jnp.int32, (2 * tq, tq), 1)
    bias_two = jnp.where((r > c) & (r <= c + tq), 0.0, NEG).astype(F32)
    bias_first = jnp.where(r <= c, 0.0, NEG).astype(F32)

    def scores(g, i):
        k0 = max(i - 1, 0) * tq
        qg = jnp.concatenate(
            [q_ref[0, (g * grp + hh) * HEAD_DIM:(g * grp + hh + 1) * HEAD_DIM, i * tq:(i + 1) * tq]
             for hh in range(grp)], axis=1)
        z = jnp.concatenate([qg, qg], axis=0)
        return jnp.dot(k_scr[g, k0:k0 + 2 * tq, :], z, preferred_element_type=F32)

    def attend(g, i, s):
        cols = slice(i * tq, (i + 1) * tq)
        k0 = max(i - 1, 0) * tq
        bias = bias_two if i > 0 else bias_first
        ps, invs = [], []
        for hh in range(grp):
            h = g * grp + hh
            sh = s[:, hh * tq:(hh + 1) * tq] + bias
            sink = sink_ref[h:h + 1, :]
            m = jnp.maximum(_colmax(sh), sink)
            p = jnp.exp(sh - m)
            den = _colsum(p) + jnp.exp(sink - m)
            ps.append(p.astype(BF16))
            invs.append(1.0 / den)
        pcat = jnp.concatenate(ps, axis=1)
        o = jnp.dot(v_ref[0, g * HEAD_DIM:(g + 1) * HEAD_DIM, k0:k0 + 2 * tq], pcat,
                    preferred_element_type=F32)
        for hh in range(grp):
            h = g * grp + hh
            o_ref[0, h * HEAD_DIM:(h + 1) * HEAD_DIM, cols] = (
                o[:, hh * tq:(hh + 1) * tq] * invs[hh]).astype(BF16)

    _emit_pipelined([(g, i) for i in range(nt) for g in range(SWA_KV_HEADS)], scores, attend)


def _swa(fm, layer, sinks_b):
    B, _, S = fm.shape
    return pl.pallas_call(
        _swa_kernel,
        grid=(B,),
        in_specs=[
            pl.BlockSpec((1, 512, S), lambda b: (b, FM_QA // 512, 0)),
            pl.BlockSpec((1, 128, S), lambda b: (b, FM_KA // 128, 0)),
            pl.BlockSpec((1, 128, S), lambda b: (b, FM_VA // 128, 0)),
            _layer_spec((SWA_Q_HEADS, SWA_TQ), layer),
        ],
        out_specs=pl.BlockSpec((1, 512, S), lambda b: (b, 0, 0)),
        out_shape=jax.ShapeDtypeStruct((B, 512, S), BF16),
        scratch_shapes=[pltpu.VMEM((SWA_KV_HEADS, S, LANES), BF16)],
        compiler_params=_cparams(("parallel",)),
        name="swa",
    )(fm, fm, fm, sinks_b)


def _moba_kernel(q_ref, k_ref, v_ref, o_ref, k_scr):
    L = MOBA_BLOCK
    S = k_ref.shape[2]
    nb = S // L
    assert nb == 8
    nbp = 16

    head_row = _iota_div((LANES, LANES), 0, HEAD_DIM)
    for j in range(S // LANES):
        kt = k_ref[0, :, j * LANES:(j + 1) * LANES].astype(F32)
        for hh in range(2):
            k_scr[hh, j * LANES:(j + 1) * LANES, :] = jnp.where(head_row == hh, kt, 0.0).T.astype(BF16)

    sel = jnp.where(_iota_div((nbp, S), 1, L) == lax.broadcasted_iota(jnp.int32, (nbp, S), 0),
                    1.0 / L, 0.0).astype(BF16)
    q = q_ref[0]
    jrow = lax.broadcasted_iota(jnp.int32, (nb, S), 0)
    qblk = _iota_div((nb, S), 1, L)
    causal = (lax.broadcasted_iota(jnp.int32, (L, L), 0) <= lax.broadcasted_iota(jnp.int32, (L, L), 1))

    biases = []
    for hh in range(2):
        kmean = jnp.dot(sel, k_scr[hh], preferred_element_type=F32)
        km_hi, km_lo = _split_bf16(kmean)
        gate = (jnp.dot(km_hi, q, preferred_element_type=F32)
                + jnp.dot(km_lo, q, preferred_element_type=F32))[:nb]
        rank = jnp.zeros((nb, S), jnp.int32)
        for d in range(1, nb):
            other = pltpu.roll(gate, d, axis=0)
            wrapped = jrow < d
            j2 = jnp.where(wrapped, jrow - d + nb, jrow - d)
            ahead = (other > gate) | ((other == gate) & jnp.logical_not(wrapped))
            rank = rank + jnp.where(ahead & (qblk > j2), 1, 0)
        biases.append(jnp.where((rank < MOBA_TOPK) & (qblk > jrow), 0.0, NEG).astype(F32))

    def scores(hh, i):
        return jnp.dot(k_scr[hh, 0:(i + 1) * L, :], q[:, i * L:(i + 1) * L], preferred_element_type=F32)

    def attend(hh, i, s_all):
        cols = slice(i * L, (i + 1) * L)
        s_own = jnp.where(causal, s_all[i * L:(i + 1) * L], NEG)
        m = _colmax(s_own)
        past = []
        for j in range(i):
            s_j = s_all[j * L:(j + 1) * L]
            b_j = biases[hh][j:j + 1, cols]
            m = jnp.maximum(m, _colmax(s_j) + b_j)
            past.append((s_j, b_j))
        vT = v_ref.at[0, hh * HEAD_DIM:(hh + 1) * HEAD_DIM, :]
        p = jnp.exp(s_own - m)
        l = _colsum(p)
        acc = jnp.dot(vT[:, cols], p.astype(BF16), preferred_element_type=F32)
        for j, (s_j, b_j) in enumerate(past):
            p = jnp.exp(s_j + (b_j - m))
            l = l + _colsum(p)
            acc = acc + jnp.dot(vT[:, j * L:(j + 1) * L], p.astype(BF16), preferred_element_type=F32)
        o_ref[0, hh * HEAD_DIM:(hh + 1) * HEAD_DIM, cols] = (acc * (1.0 / l)).astype(BF16)

    _emit_pipelined([(hh, i) for i in range(nb) for hh in range(2)], scores, attend)


def _moba(fm):
    B, _, S = fm.shape
    hp = MOBA_HEADS // 2
    return pl.pallas_call(
        _moba_kernel,
        grid=(B, hp),
        in_specs=[
            pl.BlockSpec((1, 128, S), lambda b, p: (b, FM_QB // 128 + p, 0)),
            pl.BlockSpec((1, 128, S), lambda b, p: (b, FM_KB // 128 + p, 0)),
            pl.BlockSpec((1, 128, S), lambda b, p: (b, FM_VB // 128 + p, 0)),
        ],
        out_specs=pl.BlockSpec((1, 128, S), lambda b, p: (b, p, 0)),
        out_shape=jax.ShapeDtypeStruct((B, 512, S), BF16),
        scratch_shapes=[pltpu.VMEM((2, S, LANES), BF16)],
        compiler_params=_cparams(("parallel", "parallel")),
        name="moba",
    )(fm, fm, fm)


def _gla_kernel(q_ref, kT_ref, v_ref, al_ref, alT_ref, r_ref, wa_ref, waT_ref, ba_ref, baT_ref, gn_ref,
                o_ref, state_scr):
    NB, T = q_ref.shape[0], q_ref.shape[1]
    C = GLA_CHUNK
    nc = T // C
    H = GLA_HEADS

    @pl.when(pl.program_id(1) == 0)
    def _():
        state_scr[...] = jnp.zeros_like(state_scr)

    ti = lax.broadcasted_iota(jnp.int32, (T, T), 0)
    tj = lax.broadcasted_iota(jnp.int32, (T, T), 1)
    same = _iota_div((T, T), 0, C) == _iota_div((T, T), 1, C)
    tril_bd = same & (tj <= ti)
    low = jnp.where(tril_bd, 1.0, 0.0).astype(BF16)
    upp = jnp.where(same & (ti <= tj), 1.0, 0.0).astype(BF16)
    ones_bd = jnp.where(same, 1.0, 0.0).astype(BF16)
    selc = jnp.where(_iota_div((T, nc * GLA_DV), 0, C) == _iota_div((T, nc * GLA_DV), 1, GLA_DV),
                     1.0, 0.0).astype(BF16)

    lane_in_pair = _iota_div((T, 2 * GLA_DK), 1, GLA_DK)
    blkmask = _iota_div((nc * GLA_DK, T), 0, GLA_DK) == _iota_div((nc * GLA_DK, T), 1, C)
    gn = gn_ref[...]

    def log_decay(n, d):
        z = jnp.dot(al_ref[n], wa_ref[...], preferred_element_type=F32) + ba_ref[...]
        d["la"] = _split_bf16(_log_sigmoid(z) * (1.0 / GLA_TAU))
        zT = jnp.dot(waT_ref[...], alT_ref[n], preferred_element_type=F32) + baT_ref[:, 0:1]
        d["laT"] = _split_bf16(_log_sigmoid(zT) * (1.0 / GLA_TAU))

    def decayed_operands(n, d):
        la_hi, la_lo = d["la"]
        laT_hi, laT_lo = d["laT"]
        b = jnp.dot(low, la_hi, preferred_element_type=F32) + jnp.dot(low, la_lo, preferred_element_type=F32)
        two = lambda rhs: (jnp.dot(laT_hi, rhs, preferred_element_type=F32)
                           + jnp.dot(laT_lo, rhs, preferred_element_type=F32))
        bT = two(upp)
        btotT = two(ones_bd)
        d["decay"] = jnp.exp(two(selc))
        qd = (q_ref[n].astype(F32) * QK_SCALE * jnp.exp(b)).astype(BF16)
        kT = kT_ref[n].astype(F32)
        d["kTd"] = (kT * jnp.exp(-bT)).astype(BF16)
        d["kTt"] = (kT * jnp.exp(btotT - bT)).astype(BF16)
        d["qm"] = []
        for h in range(H):
            qp = qd[:, (h // 2) * 128:(h // 2 + 1) * 128]
            d["qm"].append(jnp.where(lane_in_pair == (h % 2), qp, jnp.zeros_like(qp)))

    def state_scan(n, d):
        d["states"] = []
        for h in range(H):
            v_h = v_ref[n, :, h * GLA_DV:(h + 1) * GLA_DV]
            kt_h = d["kTt"][h * GLA_DK:(h + 1) * GLA_DK, :]
            kst = jnp.where(blkmask, jnp.concatenate([kt_h] * nc, axis=0), jnp.zeros((nc * GLA_DK, T), BF16))
            ds_all = jnp.dot(kst, v_h, preferred_element_type=F32)
            st = state_scr[n, h * GLA_DK:(h + 1) * GLA_DK, :]
            before = []
            for c in range(nc):
                before.append(st.astype(BF16))
                st = (d["decay"][h * GLA_DK:(h + 1) * GLA_DK, c * GLA_DV:(c + 1) * GLA_DV] * st
                      + ds_all[c * GLA_DK:(c + 1) * GLA_DK])
            state_scr[n, h * GLA_DK:(h + 1) * GLA_DK, :] = st
            d["states"].append(before)

    def outputs(n, d):
        for h in range(H):
            p = h // 2
            v_h = v_ref[n, :, h * GLA_DV:(h + 1) * GLA_DV]
            a = jnp.dot(d["qm"][h], d["kTd"][p * 128:(p + 1) * 128, :], preferred_element_type=F32)
            a = jnp.where(tril_bd, a, 0.0).astype(BF16)
            o = jnp.dot(a, v_h, preferred_element_type=F32)
            rhs = jnp.concatenate(
                [jnp.concatenate([d["states"][2 * p][c], d["states"][2 * p + 1][c]], axis=0) for c in range(nc)],
                axis=1)
            oi = jnp.dot(d["qm"][h], rhs, preferred_element_type=F32)
            o = o + jnp.concatenate(
                [oi[c * C:(c + 1) * C, c * GLA_DV:(c + 1) * GLA_DV] for c in range(nc)], axis=0)
            y = o * lax.rsqrt(jnp.mean(o * o, axis=-1, keepdims=True) + EPS) * gn
            r = r_ref[n, :, h * GLA_DV:(h + 1) * GLA_DV].astype(F32)
            o_ref[n, :, h * GLA_DV:(h + 1) * GLA_DV] = (y * (r * _sigmoid(r))).astype(BF16)

    work = [dict() for _ in range(NB)]
    for stage in (log_decay, decayed_operands, state_scan, outputs):
        for n in range(NB):
            stage(n, work[n])


def _gla(tok, fm, layer, wa, waT, ba, baT, gn):
    B, S, _ = tok.shape
    T = GLA_STEP
    NB = GLA_BATCH_ROWS
    assert B % NB == 0 and S % T == 0
    const = lambda shape: _layer_spec(shape, layer)
    return pl.pallas_call(
        _gla_kernel,
        grid=(B // NB, S // T),
        in_specs=[
            pl.BlockSpec((NB, T, 256), lambda b, s: (b, s, TK_QG // 256)),
            pl.BlockSpec((NB, 256, T), lambda b, s: (b, FM_KG // 256, s)),
            pl.BlockSpec((NB, T, 512), lambda b, s: (b, s, TK_VG // 512)),
            pl.BlockSpec((NB, T, 128), lambda b, s: (b, s, TK_AL // 128)),
            pl.BlockSpec((NB, 16, T), lambda b, s: (b, FM_AL // 16, s)),
            pl.BlockSpec((NB, T, 512), lambda b, s: (b, s, TK_RG // 512)),
            const((128, 256)), const((256, 16)), const((1, 256)), const((256, LANES)), const((1, GLA_DV)),
        ],
        out_specs=pl.BlockSpec((NB, T, 512), lambda b, s: (b, s, 0)),
        out_shape=jax.ShapeDtypeStruct((B, S, 512), BF16),
        scratch_shapes=[pltpu.VMEM((NB, GLA_HEADS * GLA_DK, GLA_DV), F32)],
        compiler_params=_cparams(("parallel", "arbitrary")),
        name="gla",
    )(tok, fm, tok, tok, fm, tok, wa, waT, ba, baT, gn)


def _merge_kernel(x_ref, yaT_ref, ybT_ref, yc_ref, g_ref, ws_ref, wm_ref, wg_ref, wo_ref, o_ref):
    tn = (((0,), (0,)), ((), ()))
    pa = lax.dot_general(yaT_ref[0], ws_ref[...], tn, preferred_element_type=F32)
    pb = lax.dot_general(ybT_ref[0], wm_ref[...], tn, preferred_element_type=F32)
    pc = jnp.dot(yc_ref[0], wg_ref[...], preferred_element_type=F32)
    D = D_MODEL
    merged = (_sigmoid(g_ref[0, :, 0:D].astype(F32)) * pa
              + _sigmoid(g_ref[0, :, D:2 * D].astype(F32)) * pb
              + _sigmoid(g_ref[0, :, 2 * D:3 * D].astype(F32)) * pc)
    o_ref[0] = x_ref[0] + jnp.dot(merged.astype(BF16), wo_ref[...], preferred_element_type=F32)


def _merge(x, yaT, ybT, yc, tok, layer, ws, wm, wg, wo):
    B, S, D = x.shape
    tm = MERGE_TM
    const = lambda shape: _layer_spec(shape, layer)
    return pl.pallas_call(
        _merge_kernel,
        grid=(B, S // tm),
        in_specs=[
            pl.BlockSpec((1, tm, D), lambda b, i: (b, i, 0)),
            pl.BlockSpec((1, 512, tm), lambda b, i: (b, 0, i)),
            pl.BlockSpec((1, 512, tm), lambda b, i: (b, 0, i)),
            pl.BlockSpec((1, tm, 512), lambda b, i: (b, i, 0)),
            pl.BlockSpec((1, tm, N_BRANCH * D), lambda b, i: (b, i, TK_GATES)),
            const((512, D)), const((512, D)), const((512, D)), const((D, D)),
        ],
        out_specs=pl.BlockSpec((1, tm, D), lambda b, i: (b, i, 0)),
        out_shape=jax.ShapeDtypeStruct((B, S, D), F32),
        compiler_params=_cparams(("parallel", "parallel")),
        name="merge",
    )(x, yaT, ybT, yc, tok, ws, wm, wg, wo)


def _ffn_kernel(x_ref, g_ref, wg_ref, wu_ref, wd_ref, o_ref, h_scr):
    x = x_ref[0]
    h = x * lax.rsqrt(jnp.mean(x * x, axis=-1, keepdims=True) + EPS) * g_ref[...]
    h_scr[...] = h.astype(BF16)
    acc = x
    for c0 in range(0, D_FF, FFN_CHUNK):
        c1 = c0 + FFN_CHUNK
        gt = jnp.dot(h_scr[...], wg_ref[:, c0:c1], preferred_element_type=F32)
        up = jnp.dot(h_scr[...], wu_ref[:, c0:c1], preferred_element_type=F32)
        a = (gt * _sigmoid(gt) * up).astype(BF16)
        acc = acc + jnp.dot(a, wd_ref[c0:c1, :], preferred_element_type=F32)
    o_ref[0] = acc


def _ffn(x, layer, g, wg, wu, wd):
    B, S, D = x.shape
    tm = FFN_TM
    const = lambda shape: _layer_spec(shape, layer)
    return pl.pallas_call(
        _ffn_kernel,
        grid=(B, S // tm),
        in_specs=[
            pl.BlockSpec((1, tm, D), lambda b, i: (b, i, 0)),
            const((1, D)), const((D, D_FF)), const((D, D_FF)), const((D_FF, D)),
        ],
        out_specs=pl.BlockSpec((1, tm, D), lambda b, i: (b, i, 0)),
        out_shape=jax.ShapeDtypeStruct((B, S, D), F32),
        scratch_shapes=[pltpu.VMEM((tm, D), BF16)],
        compiler_params=_cparams(("parallel", "parallel")),
        name="ffn",
    )(x, g, wg, wu, wd)


def _cols(w, *ranges):
    return jnp.concatenate([w[..., a:b] for a, b in ranges], axis=-1)


def _prep_weights(attn_norm, w_in, swa_qn, swa_kn, swa_sinks, moba_qn, moba_kn, w_alpha, b_alpha, gla_out_norm,
                  w_branch_swa, w_branch_moba, w_branch_gla, w_out, ffn_norm, w_ffn_gate, w_ffn_up, w_ffn_down):
    depth, D, _ = w_in.shape
    wfm = jnp.swapaxes(_cols(w_in, _C_QB, _C_KB, _C_VB, _C_QA, _C_KA, _C_VA, _C_KG, _C_AL), 1, 2).astype(BF16)
    al_pad = jnp.pad(w_in[..., _C_AL[0]:_C_AL[1]], ((0, 0), (0, 0), (0, LANES - GLA_GATE_RANK)))
    wtok = jnp.concatenate([_cols(w_in, _C_GATES, _C_VG, _C_RG, _C_QG), al_pad], axis=-1).astype(BF16)
    gains = jnp.pad(jnp.stack([moba_qn * QK_SCALE, moba_kn, swa_qn * QK_SCALE, swa_kn], axis=-1),
                    ((0, 0), (0, 0), (0, LANES - 4)))
    return dict(
        attn_norm=attn_norm.reshape(depth, 1, D), wtok=wtok, wfm=wfm, gains=gains,
        sinks=jnp.broadcast_to(swa_sinks[..., None], (depth, SWA_Q_HEADS, SWA_TQ)),
        wa=jnp.pad(w_alpha, ((0, 0), (0, LANES - GLA_GATE_RANK), (0, 0))).astype(BF16),
        waT=jnp.swapaxes(w_alpha, 1, 2).astype(BF16),
        ba=b_alpha[:, None, :], baT=jnp.broadcast_to(b_alpha[..., None], b_alpha.shape + (LANES,)),
        gn=gla_out_norm[:, None, :],
        ws=w_branch_swa.astype(BF16), wm=w_branch_moba.astype(BF16), wg=w_branch_gla.astype(BF16),
        wo=w_out.astype(BF16), ffn_norm=ffn_norm.reshape(depth, 1, D),
        wfg=w_ffn_gate.astype(BF16), wfu=w_ffn_up.astype(BF16), wfd=w_ffn_down.astype(BF16))


def kernel(x, positions, attn_norm, w_in, swa_q_norm, swa_k_norm, swa_sinks, moba_q_norm, moba_k_norm,
           gla_w_alpha, gla_b_alpha, gla_out_norm, w_branch_swa, w_branch_moba, w_branch_gla, w_out,
           ffn_norm, w_ffn_gate, w_ffn_up, w_ffn_down):
    B, S, D = x.shape
    depth = w_in.shape[0]
    pos3 = positions.reshape(B, 1, S)
    inv_freq = ROPE_THETA ** (-jnp.arange(0, HEAD_DIM, 2, dtype=F32) / HEAD_DIM)
    invf = jnp.broadcast_to(inv_freq.reshape(-1, 1), (HEAD_DIM // 2, LANES))
    rope = _rope_tables(pos3, invf)
    w = _prep_weights(attn_norm, w_in, swa_q_norm, swa_k_norm, swa_sinks, moba_q_norm, moba_k_norm,
                      gla_w_alpha, gla_b_alpha, gla_out_norm, w_branch_swa, w_branch_moba, w_branch_gla,
                      w_out, ffn_norm, w_ffn_gate, w_ffn_up, w_ffn_down)
    for l in range(depth):
        tok, fm = _in_proj(x, rope, l, w["attn_norm"], w["wtok"], w["wfm"], w["gains"])
        yaT = _swa(fm, l, w["sinks"])
        ybT = _moba(fm)
        yc = _gla(tok, fm, l, w["wa"], w["waT"], w["ba"], w["baT"], w["gn"])
        x = _merge(x, yaT, ybT, yc, tok, l, w["ws"], w["wm"], w["wg"], w["wo"])
        x = _ffn(x, l, w["ffn_norm"], w["wfg"], w["wfu"], w["wfd"])
    return x
```

```python
import functools

import jax
import jax.numpy as jnp
from jax import lax
from jax.experimental import pallas as pl
from jax.experimental.pallas import tpu as pltpu

F32 = jnp.float32
BF16 = jnp.bfloat16

D_MODEL = 1024
HEAD_DIM = 64
ROPE_THETA = 10000.0
EPS = 1e-6
SWA_Q_HEADS = 8
SWA_KV_HEADS = 2
SWA_WINDOW = 128
MOBA_HEADS = 8
MOBA_BLOCK = 256
MOBA_TOPK = 3
GLA_HEADS = 4
GLA_DK = 64
GLA_DV = 128
GLA_GATE_RANK = 16
GLA_TAU = 16.0
GLA_CHUNK = 64
N_BRANCH = 3
D_FF = 2816
QK_SCALE = HEAD_DIM ** -0.5
NEG = -1e30

VMEM_LIMIT_BYTES = 56 * 1024 * 1024
LANES = 128

_C_QA, _C_KA, _C_VA = (0, 512), (512, 640), (640, 768)
_C_QB, _C_KB, _C_VB = (768, 1280), (1280, 1792), (1792, 2304)
_C_QG, _C_KG, _C_VG = (2304, 2560), (2560, 2816), (2816, 3328)
_C_AL, _C_RG, _C_GATES = (3328, 3344), (3344, 3856), (3856, 6928)

FM_QB, FM_KB, FM_VB, FM_QA, FM_KA, FM_VA, FM_KG, FM_AL = 0, 512, 1024, 1536, 2048, 2176, 2304, 2560
FM_ROWS = 2576
TK_GATES, TK_VG, TK_RG, TK_QG, TK_AL = 0, 3072, 3584, 4096, 4352
TK_COLS = 4480

IN_TM = 512
POST_TM = 512
FFN_CHUNK = 256
SWA_TQ = 128
GLA_STEP = 256
GLA_BATCH_ROWS = 4


def _sigmoid(x):
    return 1.0 / (1.0 + jnp.exp(-x))


def _log_sigmoid(x):
    return jnp.minimum(x, 0.0) - jnp.log(1.0 + jnp.exp(-jnp.abs(x)))


def _iota_div(shape, dim, n):
    assert n & (n - 1) == 0
    return lax.shift_right_logical(lax.broadcasted_iota(jnp.int32, shape, dim), n.bit_length() - 1)


def _fold_rows(x, op):
    n = x.shape[0]
    if n > 32 and n % 32 == 0:
        acc = x[0:32]
        for g in range(1, n // 32):
            acc = op(acc, x[g * 32:(g + 1) * 32])
        x, n = acc, 32
    while n > 8 and n % 16 == 0:
        n //= 2
        x = op(x[:n], x[n:])
    return x


def _colmax(x):
    return jnp.max(_fold_rows(x, jnp.maximum), axis=0, keepdims=True)


def _colsum(x):
    return jnp.sum(_fold_rows(x, jnp.add), axis=0, keepdims=True)


def _emit_pipelined(units, produce, consume, lookahead=2):
    pending = [produce(*u) for u in units[:lookahead]]
    for n, unit in enumerate(units):
        cur = pending.pop(0)
        if n + lookahead < len(units):
            pending.append(produce(*units[n + lookahead]))
        consume(*unit, cur)


def _split_bf16(x):
    hi = x.astype(BF16)
    lo = (x - hi.astype(F32)).astype(BF16)
    return hi, lo


def _layer_spec(shape, layer, single_buffer=False):
    zeros = (0,) * len(shape)
    mode = dict(pipeline_mode=pl.Buffered(1)) if single_buffer else {}
    return pl.BlockSpec((None,) + tuple(shape), lambda *_: (layer,) + zeros, **mode)


def _cparams(sem):
    return pltpu.CompilerParams(dimension_semantics=sem, vmem_limit_bytes=VMEM_LIMIT_BYTES)


def _rope_kernel(pos_ref, invf_ref, o_ref):
    ang = invf_ref[:, 0:1] * pos_ref[0].astype(F32)
    o_ref[0, 0:HEAD_DIM // 2, :] = jnp.cos(ang)
    o_ref[0, HEAD_DIM // 2:, :] = jnp.sin(ang)


def _rope_tables(pos3, invf):
    B, _, S = pos3.shape
    return pl.pallas_call(
        _rope_kernel,
        grid=(B,),
        in_specs=[pl.BlockSpec((1, 1, S), lambda b: (b, 0, 0)),
                  pl.BlockSpec((HEAD_DIM // 2, LANES), lambda b: (0, 0))],
        out_specs=pl.BlockSpec((1, HEAD_DIM, S), lambda b: (b, 0, 0)),
        out_shape=jax.ShapeDtypeStruct((B, HEAD_DIM, S), F32),
        compiler_params=_cparams(("parallel",)),
        name="rope_tables",
    )(pos3, invf)


def _in_proj_kernel(x_ref, rope_ref, g_ref, wtok_ref, wfm_ref, gains_ref, tok_ref, fm_ref, h_scr):
    x = x_ref[0]
    h = x * lax.rsqrt(jnp.mean(x * x, axis=-1, keepdims=True) + EPS) * g_ref[...]
    h_scr[...] = h.astype(BF16)

    def tok_chunk(c0, c1):
        tok_ref[0, :, c0:c1] = jnp.dot(h_scr[...], wtok_ref[:, c0:c1], preferred_element_type=F32).astype(BF16)

    def fm_block(r0, nrows):
        return lax.dot_general(wfm_ref[r0:r0 + nrows, :], h_scr[...], (((1,), (1,)), ((), ())),
                               preferred_element_type=F32)

    step = 512
    tok_chunks = [(c0, min(c0 + step, TK_COLS)) for c0 in range(0, TK_COLS, step)]
    tok_chunk(*tok_chunks.pop(0))

    cos = rope_ref[0, 0:HEAD_DIM // 2, :]
    sin = rope_ref[0, HEAD_DIM // 2:, :]

    def norm_rope(a, gain_col):
        y = a * lax.rsqrt(jnp.mean(a * a, axis=0, keepdims=True) + EPS) * gains_ref[:, gain_col:gain_col + 1]
        y1, y2 = y[:HEAD_DIM // 2], y[HEAD_DIM // 2:]
        return jnp.concatenate([y1 * cos - y2 * sin, y2 * cos + y1 * sin], axis=0)

    blk = 256
    split = lambda row0, rows: [(r0, min(blk, row0 + rows - r0)) for r0 in range(row0, row0 + rows, blk)]
    for row0, rows, gcol in ((FM_QB, 512, 0), (FM_KB, 512, 1), (FM_QA, 512, 2), (FM_KA, 128, 3)):
        for r0, n in split(row0, rows):
            a = fm_block(r0, n)
            if tok_chunks:
                tok_chunk(*tok_chunks.pop(0))
            for hh in range(n // HEAD_DIM):
                fm_ref[0, r0 + hh * HEAD_DIM:r0 + (hh + 1) * HEAD_DIM, :] = norm_rope(
                    a[hh * HEAD_DIM:(hh + 1) * HEAD_DIM], gcol).astype(BF16)
    for c in tok_chunks:
        tok_chunk(*c)
    for r0, n in ((FM_VB, 512), (FM_VA, FM_ROWS - FM_VA)):
        fm_ref[0, r0:r0 + n, :] = fm_block(r0, n).astype(BF16)


def _in_proj(x, rope, layer, g, wtok, wfm, gains):
    B, S, D = x.shape
    tm = IN_TM
    const = lambda shape: _layer_spec(shape, layer)
    return pl.pallas_call(
        _in_proj_kernel,
        grid=(B, S // tm),
        in_specs=[
            pl.BlockSpec((1, tm, D), lambda b, i: (b, i, 0)),
            pl.BlockSpec((1, HEAD_DIM, tm), lambda b, i: (b, 0, i)),
            const((1, D)),
            const((D, TK_COLS)),
            const((FM_ROWS, D)),
            const((HEAD_DIM, LANES)),
        ],
        out_specs=[
            pl.BlockSpec((1, tm, TK_COLS), lambda b, i: (b, i, 0)),
            pl.BlockSpec((1, FM_ROWS, tm), lambda b, i: (b, 0, i)),
        ],
        out_shape=[
            jax.ShapeDtypeStruct((B, S, TK_COLS), BF16),
            jax.ShapeDtypeStruct((B, FM_ROWS, S), BF16),
        ],
        scratch_shapes=[pltpu.VMEM((tm, D), BF16)],
        compiler_params=_cparams(("parallel", "parallel")),
        name="in_proj",
    )(x, rope, g, wtok, wfm, gains)


def _moba_kernel(q_ref, k_ref, v_ref, o_ref, k_scr):
    L = MOBA_BLOCK
    S = k_ref.shape[2]
    nb = S // L
    assert nb == 8
    nbp = 16

    head_row = _iota_div((LANES, LANES), 0, HEAD_DIM)
    for j in range(S // LANES):
        kt = k_ref[0, :, j * LANES:(j + 1) * LANES].astype(F32)
        for hh in range(2):
            k_scr[hh, j * LANES:(j + 1) * LANES, :] = jnp.where(head_row == hh, kt, 0.0).T.astype(BF16)

    sel = jnp.where(_iota_div((nbp, S), 1, L) == lax.broadcasted_iota(jnp.int32, (nbp, S), 0),
                    1.0 / L, 0.0).astype(BF16)
    q = q_ref[0]
    jrow = lax.broadcasted_iota(jnp.int32, (nb, S), 0)
    qblk = _iota_div((nb, S), 1, L)
    causal = (lax.broadcasted_iota(jnp.int32, (L, L), 0) <= lax.broadcasted_iota(jnp.int32, (L, L), 1))

    biases = []
    for hh in range(2):
        kmean = jnp.dot(sel, k_scr[hh], preferred_element_type=F32)
        km_hi, km_lo = _split_bf16(kmean)
        gate = (jnp.dot(km_hi, q, preferred_element_type=F32)
                + jnp.dot(km_lo, q, preferred_element_type=F32))[:nb]
        rank = jnp.zeros((nb, S), jnp.int32)
        for d in range(1, nb):
            other = pltpu.roll(gate, d, axis=0)
            wrapped = jrow < d
            j2 = jnp.where(wrapped, jrow - d + nb, jrow - d)
            ahead = (other > gate) | ((other == gate) & jnp.logical_not(wrapped))
            rank = rank + jnp.where(ahead & (qblk > j2), 1, 0)
        biases.append(jnp.where((rank < MOBA_TOPK) & (qblk > jrow), 0.0, NEG).astype(F32))

    def scores(hh, i):
        return jnp.dot(k_scr[hh, 0:(i + 1) * L, :], q[:, i * L:(i + 1) * L], preferred_element_type=F32)

    def attend(hh, i, s_all):
        cols = slice(i * L, (i + 1) * L)
        s_own = jnp.where(causal, s_all[i * L:(i + 1) * L], NEG)
        m = _colmax(s_own)
        past = []
        for j in range(i):
            s_j = s_all[j * L:(j + 1) * L]
            b_j = biases[hh][j:j + 1, cols]
            m = jnp.maximum(m, _colmax(s_j) + b_j)
            past.append((s_j, b_j))
        vT = v_ref.at[0, hh * HEAD_DIM:(hh + 1) * HEAD_DIM, :]
        p = jnp.exp(s_own - m)
        l = _colsum(p)
        acc = jnp.dot(vT[:, cols], p.astype(BF16), preferred_element_type=F32)
        for j, (s_j, b_j) in enumerate(past):
            p = jnp.exp(s_j + (b_j - m))
            l = l + _colsum(p)
            acc = acc + jnp.dot(vT[:, j * L:(j + 1) * L], p.astype(BF16), preferred_element_type=F32)
        o_ref[0, hh * HEAD_DIM:(hh + 1) * HEAD_DIM, cols] = (acc * (1.0 / l)).astype(BF16)

    _emit_pipelined([(hh, i) for i in range(nb) for hh in range(2)], scores, attend)


def _moba(fm):
    B, _, S = fm.shape
    hp = MOBA_HEADS // 2
    return pl.pallas_call(
        _moba_kernel,
        grid=(B, hp),
        in_specs=[
            pl.BlockSpec((1, 128, S), lambda b, p: (b, FM_QB // 128 + p, 0)),
            pl.BlockSpec((1, 128, S), lambda b, p: (b, FM_KB // 128 + p, 0)),
            pl.BlockSpec((1, 128, S), lambda b, p: (b, FM_VB // 128 + p, 0)),
        ],
        out_specs=pl.BlockSpec((1, 128, S), lambda b, p: (b, p, 0)),
        out_shape=jax.ShapeDtypeStruct((B, 512, S), BF16),
        scratch_shapes=[pltpu.VMEM((2, S, LANES), BF16)],
        compiler_params=_cparams(("parallel", "parallel")),
        name="moba",
    )(fm, fm, fm)


def _gla_kernel(q_ref, kT_ref, v_ref, al_ref, alT_ref, r_ref, wa_ref, waT_ref, ba_ref, baT_ref, gn_ref,
                o_ref, state_scr):
    NB, T = q_ref.shape[0], q_ref.shape[1]
    C = GLA_CHUNK
    nc = T // C
    H = GLA_HEADS

    @pl.when(pl.program_id(1) == 0)
    def _():
        state_scr[...] = jnp.zeros_like(state_scr)

    ti = lax.broadcasted_iota(jnp.int32, (T, T), 0)
    tj = lax.broadcasted_iota(jnp.int32, (T, T), 1)
    same = _iota_div((T, T), 0, C) == _iota_div((T, T), 1, C)
    tril_bd = same & (tj <= ti)
    low = jnp.where(tril_bd, 1.0, 0.0).astype(BF16)
    upp = jnp.where(same & (ti <= tj), 1.0, 0.0).astype(BF16)
    ones_bd = jnp.where(same, 1.0, 0.0).astype(BF16)
    selc = jnp.where(_iota_div((T, nc * GLA_DV), 0, C) == _iota_div((T, nc * GLA_DV), 1, GLA_DV),
                     1.0, 0.0).astype(BF16)

    lane_in_pair = _iota_div((T, 2 * GLA_DK), 1, GLA_DK)
    blkmask = _iota_div((nc * GLA_DK, T), 0, GLA_DK) == _iota_div((nc * GLA_DK, T), 1, C)
    gn = gn_ref[...]

    def log_decay(n, d):
        z = jnp.dot(al_ref[n], wa_ref[...], preferred_element_type=F32) + ba_ref[...]
        d["la"] = _split_bf16(_log_sigmoid(z) * (1.0 / GLA_TAU))
        zT = jnp.dot(waT_ref[...], alT_ref[n], preferred_element_type=F32) + baT_ref[:, 0:1]
        d["laT"] = _split_bf16(_log_sigmoid(zT) * (1.0 / GLA_TAU))

    def decayed_operands(n, d):
        la_hi, la_lo = d["la"]
        laT_hi, laT_lo = d["laT"]
        b = jnp.dot(low, la_hi, preferred_element_type=F32) + jnp.dot(low, la_lo, preferred_element_type=F32)
        two = lambda rhs: (jnp.dot(laT_hi, rhs, preferred_element_type=F32)
                           + jnp.dot(laT_lo, rhs, preferred_element_type=F32))
        bT = two(upp)
        btotT = two(ones_bd)
        d["decay"] = jnp.exp(two(selc))
        qd = (q_ref[n].astype(F32) * QK_SCALE * jnp.exp(b)).astype(BF16)
        kT = kT_ref[n].astype(F32)
        d["kTd"] = (kT * jnp.exp(-bT)).astype(BF16)
        d["kTt"] = (kT * jnp.exp(btotT - bT)).astype(BF16)
        d["qm"] = []
        for h in range(H):
            qp = qd[:, (h // 2) * 128:(h // 2 + 1) * 128]
            d["qm"].append(jnp.where(lane_in_pair == (h % 2), qp, jnp.zeros_like(qp)))

    def state_scan(n, d):
        d["states"] = []
        for h in range(H):
            v_h = v_ref[n, :, h * GLA_DV:(h + 1) * GLA_DV]
            kt_h = d["kTt"][h * GLA_DK:(h + 1) * GLA_DK, :]
            kst = jnp.where(blkmask, jnp.concatenate([kt_h] * nc, axis=0), jnp.zeros((nc * GLA_DK, T), BF16))
            ds_all = jnp.dot(kst, v_h, preferred_element_type=F32)
            st = state_scr[n, h * GLA_DK:(h + 1) * GLA_DK, :]
            before = []
            for c in range(nc):
                before.append(st.astype(BF16))
                st = (d["decay"][h * GLA_DK:(h + 1) * GLA_DK, c * GLA_DV:(c + 1) * GLA_DV] * st
                      + ds_all[c * GLA_DK:(c + 1) * GLA_DK])
            state_scr[n, h * GLA_DK:(h + 1) * GLA_DK, :] = st
            d["states"].append(before)

    def outputs(n, d):
        for h in range(H):
            p = h // 2
            v_h = v_ref[n, :, h * GLA_DV:(h + 1) * GLA_DV]
            a = jnp.dot(d["qm"][h], d["kTd"][p * 128:(p + 1) * 128, :], preferred_element_type=F32)
            a = jnp.where(tril_bd, a, 0.0).astype(BF16)
            o = jnp.dot(a, v_h, preferred_element_type=F32)
            rhs = jnp.concatenate(
                [jnp.concatenate([d["states"][2 * p][c], d["states"][2 * p + 1][c]], axis=0) for c in range(nc)],
                axis=1)
            oi = jnp.dot(d["qm"][h], rhs, preferred_element_type=F32)
            o = o + jnp.concatenate(
                [oi[c * C:(c + 1) * C, c * GLA_DV:(c + 1) * GLA_DV] for c in range(nc)], axis=0)
            y = o * lax.rsqrt(jnp.mean(o * o, axis=-1, keepdims=True) + EPS) * gn
            r = r_ref[n, :, h * GLA_DV:(h + 1) * GLA_DV].astype(F32)
            o_ref[n, :, h * GLA_DV:(h + 1) * GLA_DV] = (y * (r * _sigmoid(r))).astype(BF16)

    work = [dict() for _ in range(NB)]
    for stage in (log_decay, decayed_operands, state_scan, outputs):
        for n in range(NB):
            stage(n, work[n])


def _gla(tok, fm, layer, wa, waT, ba, baT, gn):
    B, S, _ = tok.shape
    T = GLA_STEP
    NB = GLA_BATCH_ROWS
    assert B % NB == 0 and S % T == 0
    const = lambda shape: _layer_spec(shape, layer)
    return pl.pallas_call(
        _gla_kernel,
        grid=(B // NB, S // T),
        in_specs=[
            pl.BlockSpec((NB, T, 256), lambda b, s: (b, s, TK_QG // 256)),
            pl.BlockSpec((NB, 256, T), lambda b, s: (b, FM_KG // 256, s)),
            pl.BlockSpec((NB, T, 512), lambda b, s: (b, s, TK_VG // 512)),
            pl.BlockSpec((NB, T, 128), lambda b, s: (b, s, TK_AL // 128)),
            pl.BlockSpec((NB, 16, T), lambda b, s: (b, FM_AL // 16, s)),
            pl.BlockSpec((NB, T, 512), lambda b, s: (b, s, TK_RG // 512)),
            const((128, 256)), const((256, 16)), const((1, 256)), const((256, LANES)), const((1, GLA_DV)),
        ],
        out_specs=pl.BlockSpec((NB, T, 512), lambda b, s: (b, s, 0)),
        out_shape=jax.ShapeDtypeStruct((B, S, 512), BF16),
        scratch_shapes=[pltpu.VMEM((NB, GLA_HEADS * GLA_DK, GLA_DV), F32)],
        compiler_params=_cparams(("parallel", "arbitrary")),
        name="gla",
    )(tok, fm, tok, tok, fm, tok, wa, waT, ba, baT, gn)


def _post_kernel(x_ref, q_ref, kc_ref, kp_ref, vc_ref, vp_ref, sink_ref, ybT_ref, yc_ref, g_ref,
                 ws_ref, wm_ref, wg_ref, wo_ref, fn_ref, wfg_ref, wfu_ref, wfd_ref, o_ref,
                 h_scr, ya_scr, k_scr, *, tiles_per_row, num_tiles):
    n = pl.program_id(0)
    tm = x_ref.shape[1]
    D = D_MODEL
    tq = SWA_TQ
    assert tq == SWA_WINDOW
    grp = SWA_Q_HEADS // SWA_KV_HEADS
    slot_w = lax.rem(n, 2)
    slot_r = 1 - slot_w

    @pl.when(n == 0)
    def _():
        ya_scr[...] = jnp.zeros_like(ya_scr)

    head_row = _iota_div((LANES, LANES), 0, HEAD_DIM)
    for j in range(tm // LANES + 1):
        kt = (kp_ref[0] if j == 0 else kc_ref[0, :, (j - 1) * LANES:j * LANES]).astype(F32)
        for g in range(SWA_KV_HEADS):
            k_scr[g, j * LANES:(j + 1) * LANES, :] = jnp.where(head_row == g, kt, 0.0).T.astype(BF16)
    r = lax.broadcasted_iota(jnp.int32, (2 * tq, tq), 0)
    c = lax.broadcasted_iota(jnp.int32, (2 * tq, tq), 1)
    bias_two = jnp.where((r > c) & (r <= c + tq), 0.0, NEG).astype(F32)
    tile_in_row = lax.rem(jnp.minimum(n, num_tiles - 1), tiles_per_row)
    seq_start = jnp.full((2 * tq, tq), tile_in_row, jnp.int32) == 0
    bias_first = jnp.where(seq_start & (r < tq), NEG, bias_two)

    def scores(g, i):
        qg = jnp.concatenate(
            [q_ref[0, (g * grp + hh) * HEAD_DIM:(g * grp + hh + 1) * HEAD_DIM, i * tq:(i + 1) * tq]
             for hh in range(grp)], axis=1)
        z = jnp.concatenate([qg, qg], axis=0)
        return jnp.dot(k_scr[g, i * tq:(i + 2) * tq, :], z, preferred_element_type=F32)

    def attend(g, i, s):
        bias = bias_two if i > 0 else bias_first
        ps, invs = [], []
        for hh in range(grp):
            h = g * grp + hh
            sh = s[:, hh * tq:(hh + 1) * tq] + bias
            sink = sink_ref[h:h + 1, :]
            m = jnp.maximum(_colmax(sh), sink)
            p = jnp.exp(sh - m)
            den = _colsum(p) + jnp.exp(sink - m)
            ps.append(p.astype(BF16))
            invs.append(1.0 / den)
        pcat = jnp.concatenate(ps, axis=1)
        rows = slice(g * HEAD_DIM, (g + 1) * HEAD_DIM)
        vblk = (jnp.concatenate([vp_ref[0, rows, :], vc_ref[0, rows, 0:tq]], axis=1) if i == 0
                else vc_ref[0, rows, (i - 1) * tq:(i + 1) * tq])
        o = jnp.dot(vblk, pcat, preferred_element_type=F32)
        for hh in range(grp):
            h = g * grp + hh
            ya_scr[slot_w, h * HEAD_DIM:(h + 1) * HEAD_DIM, i * tq:(i + 1) * tq] = (
                o[:, hh * tq:(hh + 1) * tq] * invs[hh]).astype(BF16)

    nsub = 2
    sub = tm // nsub
    tn = (((0,), (0,)), ((), ()))

    def branches(rs, d):
        d["pa"] = lax.dot_general(ya_scr[slot_r, :, rs], ws_ref[...], tn, preferred_element_type=F32)
        d["pb"] = lax.dot_general(ybT_ref[0, :, rs], wm_ref[...], tn, preferred_element_type=F32)
        d["pc"] = jnp.dot(yc_ref[0, rs, :], wg_ref[...], preferred_element_type=F32)

    def gate_merge(rs, d):
        d["merged"] = (_sigmoid(g_ref[0, rs, 0:D].astype(F32)) * d.pop("pa")
                       + _sigmoid(g_ref[0, rs, D:2 * D].astype(F32)) * d.pop("pb")
                       + _sigmoid(g_ref[0, rs, 2 * D:3 * D].astype(F32)) * d.pop("pc")).astype(BF16)

    def out_proj(rs, d):
        d["x1"] = x_ref[0, rs, :] + jnp.dot(d.pop("merged"), wo_ref[...], preferred_element_type=F32)

    def ffn_norm(rs, d):
        x1 = d["x1"]
        h = x1 * lax.rsqrt(jnp.mean(x1 * x1, axis=-1, keepdims=True) + EPS) * fn_ref[...]
        h_scr[rs, :] = h.astype(BF16)

    work = [(slice(t * sub, (t + 1) * sub), dict()) for t in range(nsub)]
    for stage in (branches, gate_merge, out_proj, ffn_norm):
        for rs, d in work:
            stage(rs, d)

    units = [(g, i) for i in range(tm // tq) for g in range(SWA_KV_HEADS)]
    lookahead = 2
    pending = [scores(*u) for u in units[:lookahead]]
    acc = jnp.concatenate([d.pop("x1") for _, d in work], axis=0)
    for ci, c0 in enumerate(range(0, D_FF, FFN_CHUNK)):
        c1 = c0 + FFN_CHUNK
        gt = jnp.dot(h_scr[...], wfg_ref[:, c0:c1], preferred_element_type=F32)
        up = jnp.dot(h_scr[...], wfu_ref[:, c0:c1], preferred_element_type=F32)
        a = (gt * _sigmoid(gt) * up).astype(BF16)
        acc = acc + jnp.dot(a, wfd_ref[c0:c1, :], preferred_element_type=F32)
        if ci < len(units):
            cur = pending.pop(0)
            if ci + lookahead < len(units):
                pending.append(scores(*units[ci + lookahead]))
            attend(*units[ci], cur)
    assert D_FF // FFN_CHUNK >= len(units)
    o_ref[0] = acc


def _post(x, fm, ybT, yc, tok, layer, sinks, ws, wm, wg, wo, fn, wfg, wfu, wfd):
    B, S, D = x.shape
    tm = POST_TM
    assert S % tm == 0 and tm % SWA_TQ == 0
    nt = S // tm
    N = B * nt

    def swa_tile(n):
        m = jnp.minimum(n, N - 1)
        return m // nt, m % nt

    def post_tile(n):
        m = jnp.maximum(n - 1, 0)
        return m // nt, m % nt

    def prev_tokens(n):
        b, i = swa_tile(n)
        return b, jnp.maximum(i * (tm // SWA_TQ) - 1, 0)

    swa_spec = lambda rows, row_blk: pl.BlockSpec((1, rows, tm), lambda n: (swa_tile(n)[0], row_blk, swa_tile(n)[1]))
    prev_spec = lambda row_blk: pl.BlockSpec((1, 128, SWA_TQ), lambda n: (prev_tokens(n)[0], row_blk, prev_tokens(n)[1]))
    tok_major = lambda cols, col_blk: pl.BlockSpec((1, tm, cols), lambda n: (post_tile(n)[0], post_tile(n)[1], col_blk))
    const = lambda shape: _layer_spec(shape, layer, single_buffer=True)
    return pl.pallas_call(
        functools.partial(_post_kernel, tiles_per_row=nt, num_tiles=N),
        grid=(N + 1,),
        in_specs=[
            tok_major(D, 0),
            swa_spec(512, FM_QA // 512),
            swa_spec(128, FM_KA // 128), prev_spec(FM_KA // 128),
            swa_spec(128, FM_VA // 128), prev_spec(FM_VA // 128),
            const((SWA_Q_HEADS, SWA_TQ)),
            pl.BlockSpec((1, 512, tm), lambda n: (post_tile(n)[0], 0, post_tile(n)[1])),
            tok_major(512, 0),
            tok_major(N_BRANCH * D, TK_GATES),
            const((512, D)), const((512, D)), const((512, D)), const((D, D)),
            const((1, D)), const((D, D_FF)), const((D, D_FF)), const((D_FF, D)),
        ],
        out_specs=tok_major(D, 0),
        out_shape=jax.ShapeDtypeStruct((B, S, D), F32),
        scratch_shapes=[pltpu.VMEM((tm, D), BF16), pltpu.VMEM((2, 512, tm), BF16),
                        pltpu.VMEM((SWA_KV_HEADS, tm + LANES, LANES), BF16)],
        compiler_params=_cparams(("arbitrary",)),
        name="post",
    )(x, fm, fm, fm, fm, fm, sinks, ybT, yc, tok, ws, wm, wg, wo, fn, wfg, wfu, wfd)


def _cols(w, *ranges):
    return jnp.concatenate([w[..., a:b] for a, b in ranges], axis=-1)


def _prep_weights(attn_norm, w_in, swa_qn, swa_kn, swa_sinks, moba_qn, moba_kn, w_alpha, b_alpha, gla_out_norm,
                  w_branch_swa, w_branch_moba, w_branch_gla, w_out, ffn_norm, w_ffn_gate, w_ffn_up, w_ffn_down):
    depth, D, _ = w_in.shape
    wfm = jnp.swapaxes(_cols(w_in, _C_QB, _C_KB, _C_VB, _C_QA, _C_KA, _C_VA, _C_KG, _C_AL), 1, 2).astype(BF16)
    al_pad = jnp.pad(w_in[..., _C_AL[0]:_C_AL[1]], ((0, 0), (0, 0), (0, LANES - GLA_GATE_RANK)))
    wtok = jnp.concatenate([_cols(w_in, _C_GATES, _C_VG, _C_RG, _C_QG), al_pad], axis=-1).astype(BF16)
    gains = jnp.pad(jnp.stack([moba_qn * QK_SCALE, moba_kn, swa_qn * QK_SCALE, swa_kn], axis=-1),
                    ((0, 0), (0, 0), (0, LANES - 4)))
    return dict(
        attn_norm=attn_norm.reshape(depth, 1, D), wtok=wtok, wfm=wfm, gains=gains,
        sinks=jnp.broadcast_to(swa_sinks[..., None], (depth, SWA_Q_HEADS, SWA_TQ)),
        wa=jnp.pad(w_alpha, ((0, 0), (0, LANES - GLA_GATE_RANK), (0, 0))).astype(BF16),
        waT=jnp.swapaxes(w_alpha, 1, 2).astype(BF16),
        ba=b_alpha[:, None, :], baT=jnp.broadcast_to(b_alpha[..., None], b_alpha.shape + (LANES,)),
        gn=gla_out_norm[:, None, :],
        ws=w_branch_swa.astype(BF16), wm=w_branch_moba.astype(BF16), wg=w_branch_gla.astype(BF16),
        wo=w_out.astype(BF16), ffn_norm=ffn_norm.reshape(depth, 1, D),
        wfg=w_ffn_gate.astype(BF16), wfu=w_ffn_up.astype(BF16), wfd=w_ffn_down.astype(BF16))


def kernel(x, positions, attn_norm, w_in, swa_q_norm, swa_k_norm, swa_sinks, moba_q_norm, moba_k_norm,
           gla_w_alpha, gla_b_alpha, gla_out_norm, w_branch_swa, w_branch_moba, w_branch_gla, w_out,
           ffn_norm, w_ffn_gate, w_ffn_up, w_ffn_down):
    B, S, D = x.shape
    depth = w_in.shape[0]
    pos3 = positions.reshape(B, 1, S)
    inv_freq = ROPE_THETA ** (-jnp.arange(0, HEAD_DIM, 2, dtype=F32) / HEAD_DIM)
    invf = jnp.broadcast_to(inv_freq.reshape(-1, 1), (HEAD_DIM // 2, LANES))
    rope = _rope_tables(pos3, invf)
    w = _prep_weights(attn_norm, w_in, swa_q_norm, swa_k_norm, swa_sinks, moba_q_norm, moba_k_norm,
                      gla_w_alpha, gla_b_alpha, gla_out_norm, w_branch_swa, w_branch_moba, w_branch_gla,
                      w_out, ffn_norm, w_ffn_gate, w_ffn_up, w_ffn_down)
    for l in range(depth):
        tok, fm = _in_proj(x, rope, l, w["attn_norm"], w["wtok"], w["wfm"], w["gains"])
        ybT = _moba(fm)
        yc = _gla(tok, fm, l, w["wa"], w["waT"], w["ba"], w["baT"], w["gn"])
        x = _post(x, fm, ybT, yc, tok, l, w["sinks"], w["ws"], w["wm"], w["wg"], w["wo"],
                  w["ffn_norm"], w["wfg"], w["wfu"], w["wfd"])
    return x
```

```python
import functools

import jax
import jax.numpy as jnp
from jax import lax
from jax.experimental import pallas as pl
from jax.experimental.pallas import tpu as pltpu

F32 = jnp.float32
BF16 = jnp.bfloat16

D_MODEL = 1024
HEAD_DIM = 64
ROPE_THETA = 10000.0
EPS = 1e-6
SWA_Q_HEADS = 8
SWA_KV_HEADS = 2
SWA_WINDOW = 128
MOBA_HEADS = 8
MOBA_BLOCK = 256
MOBA_TOPK = 3
GLA_HEADS = 4
GLA_DK = 64
GLA_DV = 128
GLA_GATE_RANK = 16
GLA_TAU = 16.0
GLA_CHUNK = 64
N_BRANCH = 3
D_FF = 2816
QK_SCALE = HEAD_DIM ** -0.5
LOG2E = 1.4426950408889634
NEG = -1e30

VMEM_LIMIT_BYTES = 56 * 1024 * 1024
LANES = 128

_C_QA, _C_KA, _C_VA = (0, 512), (512, 640), (640, 768)
_C_QB, _C_KB, _C_VB = (768, 1280), (1280, 1792), (1792, 2304)
_C_QG, _C_KG, _C_VG = (2304, 2560), (2560, 2816), (2816, 3328)
_C_AL, _C_RG, _C_GATES = (3328, 3344), (3344, 3856), (3856, 6928)

FM_QB, FM_KB, FM_VB, FM_QA, FM_KA, FM_VA, FM_KG, FM_AL = 0, 512, 1024, 1536, 2048, 2176, 2304, 2560
FM_ROWS = 2576
TK_GATES, TK_VG, TK_RG, TK_QG = 0, 3072, 3584, 4096
TK_COLS = 4352

IN_TM = 512
POST_TM = 512
FFN_CHUNK = 256
SWA_TQ = 128
GLA_STEP = 256
GLA_BATCH_ROWS = 4


def _sigmoid(x):
    return 1.0 / (1.0 + jnp.exp(-x))


def _log_sigmoid(x):
    return jnp.minimum(x, 0.0) - jnp.log(1.0 + jnp.exp(-jnp.abs(x)))


def _iota_div(shape, dim, n):
    assert n & (n - 1) == 0
    return lax.shift_right_logical(lax.broadcasted_iota(jnp.int32, shape, dim), n.bit_length() - 1)


def _fold_rows(x, op):
    n = x.shape[0]
    if n > 32 and n % 32 == 0:
        acc = x[0:32]
        for g in range(1, n // 32):
            acc = op(acc, x[g * 32:(g + 1) * 32])
        x, n = acc, 32
    while n > 8 and n % 16 == 0:
        n //= 2
        x = op(x[:n], x[n:])
    return x


def _colmax(x):
    return jnp.max(_fold_rows(x, jnp.maximum), axis=0, keepdims=True)


def _colsum(x):
    return jnp.sum(_fold_rows(x, jnp.add), axis=0, keepdims=True)


def _emit_pipelined(units, produce, consume, lookahead=2):
    pending = [produce(*u) for u in units[:lookahead]]
    for n, unit in enumerate(units):
        cur = pending.pop(0)
        if n + lookahead < len(units):
            pending.append(produce(*units[n + lookahead]))
        consume(*unit, cur)


def _split_bf16(x):
    hi = x.astype(BF16)
    lo = (x - hi.astype(F32)).astype(BF16)
    return hi, lo


def _layer_spec(shape, layer, single_buffer=False):
    zeros = (0,) * len(shape)
    mode = dict(pipeline_mode=pl.Buffered(1)) if single_buffer else {}
    return pl.BlockSpec((None,) + tuple(shape), lambda *_: (layer,) + zeros, **mode)


def _cparams(sem):
    return pltpu.CompilerParams(dimension_semantics=sem, vmem_limit_bytes=VMEM_LIMIT_BYTES)


def _rope_kernel(pos_ref, invf_ref, o_ref):
    ang = invf_ref[:, 0:1] * pos_ref[0].astype(F32)
    o_ref[0, 0:HEAD_DIM // 2, :] = jnp.cos(ang)
    o_ref[0, HEAD_DIM // 2:, :] = jnp.sin(ang)


def _rope_tables(pos3, invf):
    B, _, S = pos3.shape
    return pl.pallas_call(
        _rope_kernel,
        grid=(B,),
        in_specs=[pl.BlockSpec((1, 1, S), lambda b: (b, 0, 0)),
                  pl.BlockSpec((HEAD_DIM // 2, LANES), lambda b: (0, 0))],
        out_specs=pl.BlockSpec((1, HEAD_DIM, S), lambda b: (b, 0, 0)),
        out_shape=jax.ShapeDtypeStruct((B, HEAD_DIM, S), F32),
        compiler_params=_cparams(("parallel",)),
        name="rope_tables",
    )(pos3, invf)


def _in_proj_kernel(x_ref, rope_ref, g_ref, wtok_ref, wfm_ref, gains_ref, tok_ref, fm_ref, h_scr):
    x = x_ref[0]
    h = x * lax.rsqrt(jnp.mean(x * x, axis=-1, keepdims=True) + EPS) * g_ref[...]
    h_scr[...] = h.astype(BF16)

    def tok_chunk(c0, c1):
        tok_ref[0, :, c0:c1] = jnp.dot(h_scr[...], wtok_ref[:, c0:c1], preferred_element_type=F32).astype(BF16)

    def fm_block(r0, nrows):
        return lax.dot_general(wfm_ref[r0:r0 + nrows, :], h_scr[...], (((1,), (1,)), ((), ())),
                               preferred_element_type=F32)

    step = 512
    tok_chunks = [(c0, min(c0 + step, TK_COLS)) for c0 in range(0, TK_COLS, step)]
    tok_chunk(*tok_chunks.pop(0))

    cos = rope_ref[0, 0:HEAD_DIM // 2, :]
    sin = rope_ref[0, HEAD_DIM // 2:, :]

    def norm_rope(a, gain_col):
        y = a * lax.rsqrt(jnp.mean(a * a, axis=0, keepdims=True) + EPS) * gains_ref[:, gain_col:gain_col + 1]
        y1, y2 = y[:HEAD_DIM // 2], y[HEAD_DIM // 2:]
        return jnp.concatenate([y1 * cos - y2 * sin, y2 * cos + y1 * sin], axis=0)

    blk = 256
    split = lambda row0, rows: [(r0, min(blk, row0 + rows - r0)) for r0 in range(row0, row0 + rows, blk)]
    for row0, rows, gcol in ((FM_QB, 512, 0), (FM_KB, 512, 1), (FM_QA, 512, 2), (FM_KA, 128, 3)):
        for r0, n in split(row0, rows):
            a = fm_block(r0, n)
            if tok_chunks:
                tok_chunk(*tok_chunks.pop(0))
            for hh in range(n // HEAD_DIM):
                fm_ref[0, r0 + hh * HEAD_DIM:r0 + (hh + 1) * HEAD_DIM, :] = norm_rope(
                    a[hh * HEAD_DIM:(hh + 1) * HEAD_DIM], gcol).astype(BF16)
    for c in tok_chunks:
        tok_chunk(*c)
    for r0, n in ((FM_VB, 512), (FM_VA, FM_ROWS - FM_VA)):
        fm_ref[0, r0:r0 + n, :] = fm_block(r0, n).astype(BF16)


def _in_proj(x, rope, layer, g, wtok, wfm, gains):
    B, S, D = x.shape
    tm = IN_TM
    const = lambda shape: _layer_spec(shape, layer)
    return pl.pallas_call(
        _in_proj_kernel,
        grid=(B, S // tm),
        in_specs=[
            pl.BlockSpec((1, tm, D), lambda b, i: (b, i, 0)),
            pl.BlockSpec((1, HEAD_DIM, tm), lambda b, i: (b, 0, i)),
            const((1, D)),
            const((D, TK_COLS)),
            const((FM_ROWS, D)),
            const((HEAD_DIM, LANES)),
        ],
        out_specs=[
            pl.BlockSpec((1, tm, TK_COLS), lambda b, i: (b, i, 0)),
            pl.BlockSpec((1, FM_ROWS, tm), lambda b, i: (b, 0, i)),
        ],
        out_shape=[
            jax.ShapeDtypeStruct((B, S, TK_COLS), BF16),
            jax.ShapeDtypeStruct((B, FM_ROWS, S), BF16),
        ],
        scratch_shapes=[pltpu.VMEM((tm, D), BF16)],
        compiler_params=_cparams(("parallel", "parallel")),
        name="in_proj",
    )(x, rope, g, wtok, wfm, gains)


def _moba_kernel(q_ref, k_ref, v_ref, o_ref, k_scr):
    L = MOBA_BLOCK
    S = k_ref.shape[2]
    nb = S // L
    assert nb == 8
    nbp = 16

    head_row = _iota_div((LANES, LANES), 0, HEAD_DIM)
    for j in range(S // LANES):
        kt = k_ref[0, :, j * LANES:(j + 1) * LANES].astype(F32)
        for hh in range(2):
            k_scr[hh, j * LANES:(j + 1) * LANES, :] = jnp.where(head_row == hh, kt, 0.0).T.astype(BF16)

    sel = jnp.where(_iota_div((nbp, S), 1, L) == lax.broadcasted_iota(jnp.int32, (nbp, S), 0),
                    1.0 / L, 0.0).astype(BF16)
    q = q_ref[0]
    jrow = lax.broadcasted_iota(jnp.int32, (nb, S), 0)
    qblk = _iota_div((nb, S), 1, L)
    causal = (lax.broadcasted_iota(jnp.int32, (L, L), 0) <= lax.broadcasted_iota(jnp.int32, (L, L), 1))

    biases = []
    for hh in range(2):
        kmean = jnp.dot(sel, k_scr[hh], preferred_element_type=F32)
        km_hi, km_lo = _split_bf16(kmean)
        gate = (jnp.dot(km_hi, q, preferred_element_type=F32)
                + jnp.dot(km_lo, q, preferred_element_type=F32))[:nb]
        rank = jnp.zeros((nb, S), jnp.int32)
        for d in range(1, nb):
            other = pltpu.roll(gate, d, axis=0)
            wrapped = jrow < d
            j2 = jnp.where(wrapped, jrow - d + nb, jrow - d)
            ahead = (other > gate) | ((other == gate) & jnp.logical_not(wrapped))
            rank = rank + jnp.where(ahead & (qblk > j2), 1, 0)
        biases.append(jnp.where((rank < MOBA_TOPK) & (qblk > jrow), 0.0, NEG).astype(F32))

    def scores(hh, i):
        return jnp.dot(k_scr[hh, 0:(i + 1) * L, :], q[:, i * L:(i + 1) * L], preferred_element_type=F32)

    def attend(hh, i, s_all):
        cols = slice(i * L, (i + 1) * L)
        s_own = jnp.where(causal, s_all[i * L:(i + 1) * L], NEG)
        m = _colmax(s_own)
        past = []
        for j in range(i):
            s_j = s_all[j * L:(j + 1) * L]
            b_j = biases[hh][j:j + 1, cols]
            m = jnp.maximum(m, _colmax(s_j) + b_j)
            past.append((s_j, b_j))
        vT = v_ref.at[0, hh * HEAD_DIM:(hh + 1) * HEAD_DIM, :]
        p = jnp.exp2(s_own - m)
        l = _colsum(p)
        acc = jnp.dot(vT[:, cols], p.astype(BF16), preferred_element_type=F32)
        for j, (s_j, b_j) in enumerate(past):
            p = jnp.exp2(s_j + (b_j - m))
            l = l + _colsum(p)
            acc = acc + jnp.dot(vT[:, j * L:(j + 1) * L], p.astype(BF16), preferred_element_type=F32)
        o_ref[0, hh * HEAD_DIM:(hh + 1) * HEAD_DIM, cols] = (acc * (1.0 / l)).astype(BF16)

    _emit_pipelined([(hh, i) for i in range(nb) for hh in range(2)], scores, attend)


def _moba(fm):
    B, _, S = fm.shape
    hp = MOBA_HEADS // 2
    return pl.pallas_call(
        _moba_kernel,
        grid=(B, hp),
        in_specs=[
            pl.BlockSpec((1, 128, S), lambda b, p: (b, FM_QB // 128 + p, 0)),
            pl.BlockSpec((1, 128, S), lambda b, p: (b, FM_KB // 128 + p, 0)),
            pl.BlockSpec((1, 128, S), lambda b, p: (b, FM_VB // 128 + p, 0)),
        ],
        out_specs=pl.BlockSpec((1, 128, S), lambda b, p: (b, p, 0)),
        out_shape=jax.ShapeDtypeStruct((B, 512, S), BF16),
        scratch_shapes=[pltpu.VMEM((2, S, LANES), BF16)],
        compiler_params=_cparams(("parallel", "parallel")),
        name="moba",
    )(fm, fm, fm)


def _gla_kernel(q_ref, kT_ref, v_ref, alT_ref, r_ref, waT_ref, baT_ref, gn_ref, o_ref, state_scr):
    NB, T = q_ref.shape[0], q_ref.shape[1]
    C = GLA_CHUNK
    nc = T // C
    H = GLA_HEADS

    @pl.when(pl.program_id(1) == 0)
    def _():
        state_scr[...] = jnp.zeros_like(state_scr)

    ti = lax.broadcasted_iota(jnp.int32, (T, T), 0)
    tj = lax.broadcasted_iota(jnp.int32, (T, T), 1)
    same = _iota_div((T, T), 0, C) == _iota_div((T, T), 1, C)
    tril_bd = same & (tj <= ti)
    upp = jnp.where(same & (ti <= tj), 1.0, 0.0).astype(BF16)

    lane_in_pair = _iota_div((T, 2 * GLA_DK), 1, GLA_DK)
    blkmask = _iota_div((nc * GLA_DK, T), 0, GLA_DK) == _iota_div((nc * GLA_DK, T), 1, C)
    gn = gn_ref[...]

    def log_decay(n, d):
        zT = jnp.dot(waT_ref[...], alT_ref[n], preferred_element_type=F32) + baT_ref[:, 0:1]
        d["laT"] = _split_bf16(_log_sigmoid(zT) * (1.0 / GLA_TAU))

    def decayed_operands(n, d):
        laT_hi, laT_lo = d["laT"]
        bT = (jnp.dot(laT_hi, upp, preferred_element_type=F32)
              + jnp.dot(laT_lo, upp, preferred_element_type=F32))
        last = [bT[:, (c + 1) * C - 1:(c + 1) * C] for c in range(nc)]
        btotT = jnp.concatenate([jnp.broadcast_to(t, (t.shape[0], C)) for t in last], axis=1)
        d["decay"] = jnp.exp(jnp.concatenate(
            [jnp.broadcast_to(t, (t.shape[0], GLA_DV)) for t in last], axis=1))
        qd = (q_ref[n].astype(F32) * QK_SCALE * jnp.exp(bT.T)).astype(BF16)
        kT = kT_ref[n].astype(F32)
        d["kTd"] = (kT * jnp.exp(-bT)).astype(BF16)
        d["kTt"] = (kT * jnp.exp(btotT - bT)).astype(BF16)
        d["qm"] = []
        for h in range(H):
            qp = qd[:, (h // 2) * 128:(h // 2 + 1) * 128]
            d["qm"].append(jnp.where(lane_in_pair == (h % 2), qp, jnp.zeros_like(qp)))

    def state_scan(n, d):
        d["states"] = []
        for h in range(H):
            v_h = v_ref[n, :, h * GLA_DV:(h + 1) * GLA_DV]
            kt_h = d["kTt"][h * GLA_DK:(h + 1) * GLA_DK, :]
            kst = jnp.where(blkmask, jnp.concatenate([kt_h] * nc, axis=0), jnp.zeros((nc * GLA_DK, T), BF16))
            ds_all = jnp.dot(kst, v_h, preferred_element_type=F32)
            st = state_scr[n, h * GLA_DK:(h + 1) * GLA_DK, :]
            before = []
            for c in range(nc):
                before.append(st.astype(BF16))
                st = (d["decay"][h * GLA_DK:(h + 1) * GLA_DK, c * GLA_DV:(c + 1) * GLA_DV] * st
                      + ds_all[c * GLA_DK:(c + 1) * GLA_DK])
            state_scr[n, h * GLA_DK:(h + 1) * GLA_DK, :] = st
            d["states"].append(before)

    def outputs(n, d):
        for h in range(H):
            p = h // 2
            v_h = v_ref[n, :, h * GLA_DV:(h + 1) * GLA_DV]
            a = jnp.dot(d["qm"][h], d["kTd"][p * 128:(p + 1) * 128, :], preferred_element_type=F32)
            a = jnp.where(tril_bd, a, 0.0).astype(BF16)
            o = jnp.dot(a, v_h, preferred_element_type=F32)
            o = o + jnp.concatenate(
                [jnp.dot(d["qm"][h][c * C:(c + 1) * C],
                         jnp.concatenate([d["states"][2 * p][c], d["states"][2 * p + 1][c]], axis=0),
                         preferred_element_type=F32) for c in range(nc)], axis=0)
            y = o * lax.rsqrt(jnp.mean(o * o, axis=-1, keepdims=True) + EPS) * gn
            r = r_ref[n, :, h * GLA_DV:(h + 1) * GLA_DV].astype(F32)
            o_ref[n, :, h * GLA_DV:(h + 1) * GLA_DV] = (y * (r * _sigmoid(r))).astype(BF16)

    work = [dict() for _ in range(NB)]
    for stage in (log_decay, decayed_operands, state_scan, outputs):
        for n in range(NB):
            stage(n, work[n])


def _gla(tok, fm, layer, waT, baT, gn):
    B, S, _ = tok.shape
    T = GLA_STEP
    NB = GLA_BATCH_ROWS
    assert B % NB == 0 and S % T == 0
    const = lambda shape: _layer_spec(shape, layer)
    return pl.pallas_call(
        _gla_kernel,
        grid=(B // NB, S // T),
        in_specs=[
            pl.BlockSpec((NB, T, 256), lambda b, s: (b, s, TK_QG // 256)),
            pl.BlockSpec((NB, 256, T), lambda b, s: (b, FM_KG // 256, s)),
            pl.BlockSpec((NB, T, 512), lambda b, s: (b, s, TK_VG // 512)),
            pl.BlockSpec((NB, 16, T), lambda b, s: (b, FM_AL // 16, s)),
            pl.BlockSpec((NB, T, 512), lambda b, s: (b, s, TK_RG // 512)),
            const((256, 16)), const((256, LANES)), const((1, GLA_DV)),
        ],
        out_specs=pl.BlockSpec((NB, T, 512), lambda b, s: (b, s, 0)),
        out_shape=jax.ShapeDtypeStruct((B, S, 512), BF16),
        scratch_shapes=[pltpu.VMEM((NB, GLA_HEADS * GLA_DK, GLA_DV), F32)],
        compiler_params=_cparams(("parallel", "arbitrary")),
        name="gla",
    )(tok, fm, tok, fm, tok, waT, baT, gn)


def _post_kernel(x_ref, q_ref, kc_ref, kp_ref, vc_ref, vp_ref, sink_ref, ybT_ref, yc_ref, g_ref,
                 ws_ref, wm_ref, wg_ref, wo_ref, fn_ref, wfg_ref, wfu_ref, wfd_ref, o_ref,
                 h_scr, ya_scr, k_scr, *, tiles_per_row, num_tiles):
    n = pl.program_id(0)
    tm = x_ref.shape[1]
    D = D_MODEL
    tq = SWA_TQ
    assert tq == SWA_WINDOW
    grp = SWA_Q_HEADS // SWA_KV_HEADS
    slot_w = lax.rem(n, 2)
    slot_r = 1 - slot_w

    @pl.when(n == 0)
    def _():
        ya_scr[...] = jnp.zeros_like(ya_scr)

    head_row = _iota_div((LANES, LANES), 0, HEAD_DIM)
    for j in range(tm // LANES + 1):
        kt = (kp_ref[0] if j == 0 else kc_ref[0, :, (j - 1) * LANES:j * LANES]).astype(F32)
        for g in range(SWA_KV_HEADS):
            k_scr[g, j * LANES:(j + 1) * LANES, :] = jnp.where(head_row == g, kt, 0.0).T.astype(BF16)
    r = lax.broadcasted_iota(jnp.int32, (2 * tq, tq), 0)
    c = lax.broadcasted_iota(jnp.int32, (2 * tq, tq), 1)
    bias_two = jnp.where((r > c) & (r <= c + tq), 0.0, NEG).astype(F32)
    tile_in_row = lax.rem(jnp.minimum(n, num_tiles - 1), tiles_per_row)
    seq_start = jnp.full((2 * tq, tq), tile_in_row, jnp.int32) == 0
    bias_first = jnp.where(seq_start & (r < tq), NEG, bias_two)

    def scores(g, i):
        qg = jnp.concatenate(
            [q_ref[0, (g * grp + hh) * HEAD_DIM:(g * grp + hh + 1) * HEAD_DIM, i * tq:(i + 1) * tq]
             for hh in range(grp)], axis=1)
        z = jnp.concatenate([qg, qg], axis=0)
        return jnp.dot(k_scr[g, i * tq:(i + 2) * tq, :], z, preferred_element_type=F32)

    def attend(g, i, s):
        bias = bias_two if i > 0 else bias_first
        ps, invs = [], []
        for hh in range(grp):
            h = g * grp + hh
            sh = s[:, hh * tq:(hh + 1) * tq] + bias
            sink = sink_ref[h:h + 1, :]
            m = jnp.maximum(_colmax(sh), sink)
            p = jnp.exp2(sh - m)
            den = _colsum(p) + jnp.exp2(sink - m)
            ps.append(p.astype(BF16))
            invs.append(1.0 / den)
        pcat = jnp.concatenate(ps, axis=1)
        rows = slice(g * HEAD_DIM, (g + 1) * HEAD_DIM)
        vblk = (jnp.concatenate([vp_ref[0, rows, :], vc_ref[0, rows, 0:tq]], axis=1) if i == 0
                else vc_ref[0, rows, (i - 1) * tq:(i + 1) * tq])
        o = jnp.dot(vblk, pcat, preferred_element_type=F32)
        for hh in range(grp):
            h = g * grp + hh
            ya_scr[slot_w, h * HEAD_DIM:(h + 1) * HEAD_DIM, i * tq:(i + 1) * tq] = (
                o[:, hh * tq:(hh + 1) * tq] * invs[hh]).astype(BF16)

    nsub = 2
    sub = tm // nsub
    tn = (((0,), (0,)), ((), ()))

    def branches(rs, d):
        d["pa"] = lax.dot_general(ya_scr[slot_r, :, rs], ws_ref[...], tn, preferred_element_type=F32)
        d["pb"] = lax.dot_general(ybT_ref[0, :, rs], wm_ref[...], tn, preferred_element_type=F32)
        d["pc"] = jnp.dot(yc_ref[0, rs, :], wg_ref[...], preferred_element_type=F32)

    def gate_merge(rs, d):
        d["merged"] = (_sigmoid(g_ref[0, rs, 0:D].astype(F32)) * d.pop("pa")
                       + _sigmoid(g_ref[0, rs, D:2 * D].astype(F32)) * d.pop("pb")
                       + _sigmoid(g_ref[0, rs, 2 * D:3 * D].astype(F32)) * d.pop("pc")).astype(BF16)

    def out_proj(rs, d):
        d["x1"] = x_ref[0, rs, :] + jnp.dot(d.pop("merged"), wo_ref[...], preferred_element_type=F32)

    def ffn_norm(rs, d):
        x1 = d["x1"]
        h = x1 * lax.rsqrt(jnp.mean(x1 * x1, axis=-1, keepdims=True) + EPS) * fn_ref[...]
        h_scr[rs, :] = h.astype(BF16)

    work = [(slice(t * sub, (t + 1) * sub), dict()) for t in range(nsub)]
    for stage in (branches, gate_merge, out_proj, ffn_norm):
        for rs, d in work:
            stage(rs, d)

    units = [(g, i) for i in range(tm // tq) for g in range(SWA_KV_HEADS)]
    lookahead = 2
    pending = [scores(*u) for u in units[:lookahead]]
    acc = jnp.concatenate([d.pop("x1") for _, d in work], axis=0)
    for ci, c0 in enumerate(range(0, D_FF, FFN_CHUNK)):
        c1 = c0 + FFN_CHUNK
        gt = jnp.dot(h_scr[...], wfg_ref[:, c0:c1], preferred_element_type=F32)
        up = jnp.dot(h_scr[...], wfu_ref[:, c0:c1], preferred_element_type=F32)
        a = (gt * _sigmoid(gt) * up).astype(BF16)
        acc = acc + jnp.dot(a, wfd_ref[c0:c1, :], preferred_element_type=F32)
        if ci < len(units):
            cur = pending.pop(0)
            if ci + lookahead < len(units):
                pending.append(scores(*units[ci + lookahead]))
            attend(*units[ci], cur)
    assert D_FF // FFN_CHUNK >= len(units)
    o_ref[0] = acc


def _post(x, fm, ybT, yc, tok, layer, sinks, ws, wm, wg, wo, fn, wfg, wfu, wfd):
    B, S, D = x.shape
    tm = POST_TM
    assert S % tm == 0 and tm % SWA_TQ == 0
    nt = S // tm
    N = B * nt

    def swa_tile(n):
        m = jnp.minimum(n, N - 1)
        return m // nt, m % nt

    def post_tile(n):
        m = jnp.maximum(n - 1, 0)
        return m // nt, m % nt

    def prev_tokens(n):
        b, i = swa_tile(n)
        return b, jnp.maximum(i * (tm // SWA_TQ) - 1, 0)

    swa_spec = lambda rows, row_blk: pl.BlockSpec((1, rows, tm), lambda n: (swa_tile(n)[0], row_blk, swa_tile(n)[1]))
    prev_spec = lambda row_blk: pl.BlockSpec((1, 128, SWA_TQ), lambda n: (prev_tokens(n)[0], row_blk, prev_tokens(n)[1]))
    tok_major = lambda cols, col_blk: pl.BlockSpec((1, tm, cols), lambda n: (post_tile(n)[0], post_tile(n)[1], col_blk))
    const = lambda shape: _layer_spec(shape, layer, single_buffer=True)
    return pl.pallas_call(
        functools.partial(_post_kernel, tiles_per_row=nt, num_tiles=N),
        grid=(N + 1,),
        in_specs=[
            tok_major(D, 0),
            swa_spec(512, FM_QA // 512),
            swa_spec(128, FM_KA // 128), prev_spec(FM_KA // 128),
            swa_spec(128, FM_VA // 128), prev_spec(FM_VA // 128),
            const((SWA_Q_HEADS, SWA_TQ)),
            pl.BlockSpec((1, 512, tm), lambda n: (post_tile(n)[0], 0, post_tile(n)[1])),
            tok_major(512, 0),
            tok_major(N_BRANCH * D, TK_GATES),
            const((512, D)), const((512, D)), const((512, D)), const((D, D)),
            const((1, D)), const((D, D_FF)), const((D, D_FF)), const((D_FF, D)),
        ],
        out_specs=tok_major(D, 0),
        out_shape=jax.ShapeDtypeStruct((B, S, D), F32),
        scratch_shapes=[pltpu.VMEM((tm, D), BF16), pltpu.VMEM((2, 512, tm), BF16),
                        pltpu.VMEM((SWA_KV_HEADS, tm + LANES, LANES), BF16)],
        compiler_params=_cparams(("arbitrary",)),
        name="post",
    )(x, fm, fm, fm, fm, fm, sinks, ybT, yc, tok, ws, wm, wg, wo, fn, wfg, wfu, wfd)


def _cols(w, *ranges):
    return jnp.concatenate([w[..., a:b] for a, b in ranges], axis=-1)


def _prep_weights(attn_norm, w_in, swa_qn, swa_kn, swa_sinks, moba_qn, moba_kn, w_alpha, b_alpha, gla_out_norm,
                  w_branch_swa, w_branch_moba, w_branch_gla, w_out, ffn_norm, w_ffn_gate, w_ffn_up, w_ffn_down):
    depth, D, _ = w_in.shape
    wfm = jnp.swapaxes(_cols(w_in, _C_QB, _C_KB, _C_VB, _C_QA, _C_KA, _C_VA, _C_KG, _C_AL), 1, 2).astype(BF16)
    wtok = _cols(w_in, _C_GATES, _C_VG, _C_RG, _C_QG).astype(BF16)
    q_fold = QK_SCALE * LOG2E
    gains = jnp.pad(jnp.stack([moba_qn * q_fold, moba_kn, swa_qn * q_fold, swa_kn], axis=-1),
                    ((0, 0), (0, 0), (0, LANES - 4)))
    return dict(
        attn_norm=attn_norm.reshape(depth, 1, D), wtok=wtok, wfm=wfm, gains=gains,
        sinks=jnp.broadcast_to((swa_sinks * LOG2E)[..., None], (depth, SWA_Q_HEADS, SWA_TQ)),
        waT=jnp.swapaxes(w_alpha, 1, 2).astype(BF16),
        baT=jnp.broadcast_to(b_alpha[..., None], b_alpha.shape + (LANES,)),
        gn=gla_out_norm[:, None, :],
        ws=w_branch_swa.astype(BF16), wm=w_branch_moba.astype(BF16), wg=w_branch_gla.astype(BF16),
        wo=w_out.astype(BF16), ffn_norm=ffn_norm.reshape(depth, 1, D),
        wfg=w_ffn_gate.astype(BF16), wfu=w_ffn_up.astype(BF16), wfd=w_ffn_down.astype(BF16))


def kernel(x, positions, attn_norm, w_in, swa_q_norm, swa_k_norm, swa_sinks, moba_q_norm, moba_k_norm,
           gla_w_alpha, gla_b_alpha, gla_out_norm, w_branch_swa, w_branch_moba, w_branch_gla, w_out,
           ffn_norm, w_ffn_gate, w_ffn_up, w_ffn_down):
    B, S, D = x.shape
    depth = w_in.shape[0]
    pos3 = positions.reshape(B, 1, S)
    inv_freq = ROPE_THETA ** (-jnp.arange(0, HEAD_DIM, 2, dtype=F32) / HEAD_DIM)
    invf = jnp.broadcast_to(inv_freq.reshape(-1, 1), (HEAD_DIM // 2, LANES))
    rope = _rope_tables(pos3, invf)
    w = _prep_weights(attn_norm, w_in, swa_q_norm, swa_k_norm, swa_sinks, moba_q_norm, moba_k_norm,
                      gla_w_alpha, gla_b_alpha, gla_out_norm, w_branch_swa, w_branch_moba, w_branch_gla,
                      w_out, ffn_norm, w_ffn_gate, w_ffn_up, w_ffn_down)
    for l in range(depth):
        tok, fm = _in_proj(x, rope, l, w["attn_norm"], w["wtok"], w["wfm"], w["gains"])
        ybT = _moba(fm)
        yc = _gla(tok, fm, l, w["waT"], w["baT"], w["gn"])
        x = _post(x, fm, ybT, yc, tok, l, w["sinks"], w["ws"], w["wm"], w["wg"], w["wo"],
                  w["ffn_norm"], w["wfg"], w["wfu"], w["wfd"])
    return x
```

```python
import functools

import jax
import jax.numpy as jnp
from jax import lax
from jax.experimental import pallas as pl
from jax.experimental.pallas import tpu as pltpu

F32 = jnp.float32
BF16 = jnp.bfloat16

D_MODEL = 1024
HEAD_DIM = 64
ROPE_THETA = 10000.0
EPS = 1e-6
SWA_Q_HEADS = 8
SWA_KV_HEADS = 2
SWA_WINDOW = 128
MOBA_HEADS = 8
MOBA_BLOCK = 256
MOBA_TOPK = 3
GLA_HEADS = 4
GLA_DK = 64
GLA_DV = 128
GLA_GATE_RANK = 16
GLA_TAU = 16.0
GLA_CHUNK = 64
N_BRANCH = 3
D_FF = 2816
QK_SCALE = HEAD_DIM ** -0.5
LOG2E = 1.4426950408889634
NEG = -1e30

VMEM_LIMIT_BYTES = 56 * 1024 * 1024
LANES = 128

_C_QA, _C_KA, _C_VA = (0, 512), (512, 640), (640, 768)
_C_QB, _C_KB, _C_VB = (768, 1280), (1280, 1792), (1792, 2304)
_C_QG, _C_KG, _C_VG = (2304, 2560), (2560, 2816), (2816, 3328)
_C_AL, _C_RG, _C_GATES = (3328, 3344), (3344, 3856), (3856, 6928)

FM_QB, FM_KB, FM_VB, FM_QA, FM_KA, FM_VA, FM_KG, FM_AL = 0, 512, 1024, 1536, 2048, 2176, 2304, 2560
FM_ROWS = 2576
TK_GATES, TK_VG, TK_RG, TK_QG = 0, 3072, 3584, 4096
TK_COLS = 4352

IN_TM = 512
POST_TM = 512
FFN_CHUNK = 256
SWA_TQ = 128
MOBA_KEY_STEP = 128
V_AUG_ROWS = HEAD_DIM + 16
GLA_STEP = 256
GLA_BATCH_ROWS = 4


def _sigmoid(x):
    return 1.0 / (1.0 + jnp.exp(-x))


def _log_sigmoid(x):
    return jnp.minimum(x, 0.0) - jnp.log(1.0 + jnp.exp(-jnp.abs(x)))


def _iota_div(shape, dim, n):
    assert n & (n - 1) == 0
    return lax.shift_right_logical(lax.broadcasted_iota(jnp.int32, shape, dim), n.bit_length() - 1)


def _fold_rows(x, op):
    n = x.shape[0]
    if n > 32 and n % 32 == 0:
        acc = x[0:32]
        for g in range(1, n // 32):
            acc = op(acc, x[g * 32:(g + 1) * 32])
        x, n = acc, 32
    while n > 8 and n % 16 == 0:
        n //= 2
        x = op(x[:n], x[n:])
    return x


def _colmax(x):
    return jnp.max(_fold_rows(x, jnp.maximum), axis=0, keepdims=True)


def _colsum(x):
    return jnp.sum(_fold_rows(x, jnp.add), axis=0, keepdims=True)


def _split_bf16(x):
    hi = x.astype(BF16)
    lo = (x - hi.astype(F32)).astype(BF16)
    return hi, lo


def _layer_spec(shape, layer, single_buffer=False):
    zeros = (0,) * len(shape)
    mode = dict(pipeline_mode=pl.Buffered(1)) if single_buffer else {}
    return pl.BlockSpec((None,) + tuple(shape), lambda *_: (layer,) + zeros, **mode)


def _cparams(sem):
    return pltpu.CompilerParams(dimension_semantics=sem, vmem_limit_bytes=VMEM_LIMIT_BYTES)


def _rope_kernel(pos_ref, invf_ref, o_ref):
    ang = invf_ref[:, 0:1] * pos_ref[0].astype(F32)
    o_ref[0, 0:HEAD_DIM // 2, :] = jnp.cos(ang)
    o_ref[0, HEAD_DIM // 2:, :] = jnp.sin(ang)


def _rope_tables(pos3, invf):
    B, _, S = pos3.shape
    return pl.pallas_call(
        _rope_kernel,
        grid=(B,),
        in_specs=[pl.BlockSpec((1, 1, S), lambda b: (b, 0, 0)),
                  pl.BlockSpec((HEAD_DIM // 2, LANES), lambda b: (0, 0))],
        out_specs=pl.BlockSpec((1, HEAD_DIM, S), lambda b: (b, 0, 0)),
        out_shape=jax.ShapeDtypeStruct((B, HEAD_DIM, S), F32),
        compiler_params=_cparams(("parallel",)),
        name="rope_tables",
    )(pos3, invf)


def _in_proj_kernel(x_ref, rope_ref, g_ref, wtok_ref, wfm_ref, gains_ref, tok_ref, fm_ref, h_scr):
    x = x_ref[0]
    h = x * lax.rsqrt(jnp.mean(x * x, axis=-1, keepdims=True) + EPS) * g_ref[...]
    h_scr[...] = h.astype(BF16)

    def tok_chunk(c0, c1):
        tok_ref[0, :, c0:c1] = jnp.dot(h_scr[...], wtok_ref[:, c0:c1], preferred_element_type=F32).astype(BF16)

    def fm_block(r0, nrows):
        return lax.dot_general(wfm_ref[r0:r0 + nrows, :], h_scr[...], (((1,), (1,)), ((), ())),
                               preferred_element_type=F32)

    step = 512
    tok_chunks = [(c0, min(c0 + step, TK_COLS)) for c0 in range(0, TK_COLS, step)]
    tok_chunk(*tok_chunks.pop(0))

    cos = rope_ref[0, 0:HEAD_DIM // 2, :]
    sin = rope_ref[0, HEAD_DIM // 2:, :]

    def norm_rope(a, gain_col):
        y = a * lax.rsqrt(jnp.mean(a * a, axis=0, keepdims=True) + EPS) * gains_ref[:, gain_col:gain_col + 1]
        y1, y2 = y[:HEAD_DIM // 2], y[HEAD_DIM // 2:]
        return jnp.concatenate([y1 * cos - y2 * sin, y2 * cos + y1 * sin], axis=0)

    blk = 256
    split = lambda row0, rows: [(r0, min(blk, row0 + rows - r0)) for r0 in range(row0, row0 + rows, blk)]
    for row0, rows, gcol in ((FM_QB, 512, 0), (FM_KB, 512, 1), (FM_QA, 512, 2), (FM_KA, 128, 3)):
        for r0, n in split(row0, rows):
            a = fm_block(r0, n)
            if tok_chunks:
                tok_chunk(*tok_chunks.pop(0))
            for hh in range(n // HEAD_DIM):
                fm_ref[0, r0 + hh * HEAD_DIM:r0 + (hh + 1) * HEAD_DIM, :] = norm_rope(
                    a[hh * HEAD_DIM:(hh + 1) * HEAD_DIM], gcol).astype(BF16)
    for c in tok_chunks:
        tok_chunk(*c)
    for r0, n in ((FM_VB, 512), (FM_VA, FM_ROWS - FM_VA)):
        fm_ref[0, r0:r0 + n, :] = fm_block(r0, n).astype(BF16)


def _in_proj(x, rope, layer, g, wtok, wfm, gains):
    B, S, D = x.shape
    tm = IN_TM
    const = lambda shape: _layer_spec(shape, layer)
    return pl.pallas_call(
        _in_proj_kernel,
        grid=(B, S // tm),
        in_specs=[
            pl.BlockSpec((1, tm, D), lambda b, i: (b, i, 0)),
            pl.BlockSpec((1, HEAD_DIM, tm), lambda b, i: (b, 0, i)),
            const((1, D)),
            const((D, TK_COLS)),
            const((FM_ROWS, D)),
            const((HEAD_DIM, LANES)),
        ],
        out_specs=[
            pl.BlockSpec((1, tm, TK_COLS), lambda b, i: (b, i, 0)),
            pl.BlockSpec((1, FM_ROWS, tm), lambda b, i: (b, 0, i)),
        ],
        out_shape=[
            jax.ShapeDtypeStruct((B, S, TK_COLS), BF16),
            jax.ShapeDtypeStruct((B, FM_ROWS, S), BF16),
        ],
        scratch_shapes=[pltpu.VMEM((tm, D), BF16)],
        compiler_params=_cparams(("parallel", "parallel")),
        name="in_proj",
    )(x, rope, g, wtok, wfm, gains)


def _moba_kernel(q_ref, k_ref, v_ref, o_ref, k_scr, v_scr):
    L = MOBA_BLOCK
    S = k_ref.shape[2]
    nb = S // L
    assert nb == 8
    nbp = 16

    head_row = _iota_div((LANES, LANES), 0, HEAD_DIM)
    for j in range(S // LANES):
        kt = k_ref[0, :, j * LANES:(j + 1) * LANES].astype(F32)
        for hh in range(2):
            k_scr[hh, j * LANES:(j + 1) * LANES, :] = jnp.where(head_row == hh, kt, 0.0).T.astype(BF16)
    for hh in range(2):
        v_scr[hh, 0:HEAD_DIM, :] = v_ref[0, hh * HEAD_DIM:(hh + 1) * HEAD_DIM, :]
        v_scr[hh, HEAD_DIM:, :] = jnp.ones((V_AUG_ROWS - HEAD_DIM, S), BF16)

    sel = jnp.where(_iota_div((nbp, S), 1, L) == lax.broadcasted_iota(jnp.int32, (nbp, S), 0),
                    1.0 / L, 0.0).astype(BF16)
    q = q_ref[0]
    jrow = lax.broadcasted_iota(jnp.int32, (nb, S), 0)
    qblk = _iota_div((nb, S), 1, L)
    causal = (lax.broadcasted_iota(jnp.int32, (L, L), 0) <= lax.broadcasted_iota(jnp.int32, (L, L), 1))

    biases = []
    for hh in range(2):
        kmean = jnp.dot(sel, k_scr[hh], preferred_element_type=F32)
        km_hi, km_lo = _split_bf16(kmean)
        gate = (jnp.dot(km_hi, q, preferred_element_type=F32)
                + jnp.dot(km_lo, q, preferred_element_type=F32))[:nb]
        rank = jnp.zeros((nb, S), jnp.int32)
        for d in range(1, nb):
            other = pltpu.roll(gate, d, axis=0)
            wrapped = jrow < d
            j2 = jnp.where(wrapped, jrow - d + nb, jrow - d)
            ahead = (other > gate) | ((other == gate) & jnp.logical_not(wrapped))
            rank = rank + jnp.where(ahead & (qblk > j2), 1, 0)
        biases.append(jnp.where((rank < MOBA_TOPK) & (qblk > jrow), 0.0, NEG).astype(F32))

    KS = MOBA_KEY_STEP
    lookahead = 8

    def scores(hh, i, j, r):
        k0 = j * L + r * KS
        return jnp.dot(k_scr[hh, k0:k0 + KS, :], q[:, i * L:(i + 1) * L], preferred_element_type=F32)

    def step(hh, i, j, r, s, st):
        k0 = j * L + r * KS
        if j == i:
            s = jnp.where(causal[r * KS:(r + 1) * KS], s, NEG)
            bm = _colmax(s)
            shift = None
        else:
            shift = biases[hh][j:j + 1, i * L:(i + 1) * L]
            bm = _colmax(s) + shift
        vblk = v_scr[hh, :, k0:k0 + KS]
        if not st:
            p = jnp.exp2(s - bm)
            st.update(m=bm, acc=jnp.dot(vblk, p.astype(BF16), preferred_element_type=F32))
            return
        m = jnp.maximum(st["m"], bm)
        alpha = jnp.exp2(st["m"] - m)
        p = jnp.exp2(s - m) if shift is None else jnp.exp2(s + (shift - m))
        st["acc"] = alpha * st["acc"] + jnp.dot(vblk, p.astype(BF16), preferred_element_type=F32)
        st["m"] = m

    chains = [((hh, i), [(j, r) for j in [i] + list(range(i)) for r in range(L // KS)], {})
              for i in range(nb) for hh in range(2)]
    order = sorted(((t + 0.5) / len(steps), c, jr) for c, (_, steps, _) in enumerate(chains)
                   for t, jr in enumerate(steps))
    order = [(c, jr) for _, c, jr in order]
    pending = [scores(*chains[c][0], *jr) for c, jr in order[:lookahead]]
    for n, (c, jr) in enumerate(order):
        s_cur = pending.pop(0)
        if n + lookahead < len(order):
            c2, jr2 = order[n + lookahead]
            pending.append(scores(*chains[c2][0], *jr2))
        (hh, i), steps, st = chains[c]
        step(hh, i, *jr, s_cur, st)
        if jr == steps[-1]:
            o_ref[0, hh * HEAD_DIM:(hh + 1) * HEAD_DIM, i * L:(i + 1) * L] = (
                st["acc"][:HEAD_DIM] * (1.0 / st["acc"][HEAD_DIM:HEAD_DIM + 1])).astype(BF16)


def _moba(fm):
    B, _, S = fm.shape
    hp = MOBA_HEADS // 2
    return pl.pallas_call(
        _moba_kernel,
        grid=(B, hp),
        in_specs=[
            pl.BlockSpec((1, 128, S), lambda b, p: (b, FM_QB // 128 + p, 0)),
            pl.BlockSpec((1, 128, S), lambda b, p: (b, FM_KB // 128 + p, 0)),
            pl.BlockSpec((1, 128, S), lambda b, p: (b, FM_VB // 128 + p, 0)),
        ],
        out_specs=pl.BlockSpec((1, 128, S), lambda b, p: (b, p, 0)),
        out_shape=jax.ShapeDtypeStruct((B, 512, S), BF16),
        scratch_shapes=[pltpu.VMEM((2, S, LANES), BF16), pltpu.VMEM((2, V_AUG_ROWS, S), BF16)],
        compiler_params=_cparams(("parallel", "parallel")),
        name="moba",
    )(fm, fm, fm)


def _gla_kernel(q_ref, kT_ref, v_ref, alT_ref, r_ref, waT_ref, baT_ref, gn_ref, o_ref, state_scr):
    NB, T = q_ref.shape[0], q_ref.shape[1]
    C = GLA_CHUNK
    nc = T // C
    H = GLA_HEADS

    @pl.when(pl.program_id(1) == 0)
    def _():
        state_scr[...] = jnp.zeros_like(state_scr)

    ti = lax.broadcasted_iota(jnp.int32, (T, T), 0)
    tj = lax.broadcasted_iota(jnp.int32, (T, T), 1)
    same = _iota_div((T, T), 0, C) == _iota_div((T, T), 1, C)
    tril_bd = same & (tj <= ti)
    upp = jnp.where(same & (ti <= tj), 1.0, 0.0).astype(BF16)

    lane_in_pair = _iota_div((T, 2 * GLA_DK), 1, GLA_DK)
    blkmask = _iota_div((nc * GLA_DK, T), 0, GLA_DK) == _iota_div((nc * GLA_DK, T), 1, C)
    gn = gn_ref[...]

    def log_decay(n, d):
        zT = jnp.dot(waT_ref[...], alT_ref[n], preferred_element_type=F32) + baT_ref[:, 0:1]
        d["laT"] = _split_bf16(_log_sigmoid(zT) * (1.0 / GLA_TAU))

    def decayed_operands(n, d):
        laT_hi, laT_lo = d["laT"]
        bT = (jnp.dot(laT_hi, upp, preferred_element_type=F32)
              + jnp.dot(laT_lo, upp, preferred_element_type=F32))
        last = [bT[:, (c + 1) * C - 1:(c + 1) * C] for c in range(nc)]
        btotT = jnp.concatenate([jnp.broadcast_to(t, (t.shape[0], C)) for t in last], axis=1)
        d["decay"] = jnp.exp(jnp.concatenate(
            [jnp.broadcast_to(t, (t.shape[0], GLA_DV)) for t in last], axis=1))
        qd = (q_ref[n].astype(F32) * QK_SCALE * jnp.exp(bT.T)).astype(BF16)
        kT = kT_ref[n].astype(F32)
        d["kTd"] = (kT * jnp.exp(-bT)).astype(BF16)
        d["kTt"] = (kT * jnp.exp(btotT - bT)).astype(BF16)
        d["qm"] = []
        for h in range(H):
            qp = qd[:, (h // 2) * 128:(h // 2 + 1) * 128]
            d["qm"].append(jnp.where(lane_in_pair == (h % 2), qp, jnp.zeros_like(qp)))

    def state_scan(n, d):
        d["states"] = []
        for h in range(H):
            v_h = v_ref[n, :, h * GLA_DV:(h + 1) * GLA_DV]
            kt_h = d["kTt"][h * GLA_DK:(h + 1) * GLA_DK, :]
            kst = jnp.where(blkmask, jnp.concatenate([kt_h] * nc, axis=0), jnp.zeros((nc * GLA_DK, T), BF16))
            ds_all = jnp.dot(kst, v_h, preferred_element_type=F32)
            st = state_scr[n, h * GLA_DK:(h + 1) * GLA_DK, :]
            before = []
            for c in range(nc):
                before.append(st.astype(BF16))
                st = (d["decay"][h * GLA_DK:(h + 1) * GLA_DK, c * GLA_DV:(c + 1) * GLA_DV] * st
                      + ds_all[c * GLA_DK:(c + 1) * GLA_DK])
            state_scr[n, h * GLA_DK:(h + 1) * GLA_DK, :] = st
            d["states"].append(before)

    def outputs(n, d):
        for h in range(H):
            p = h // 2
            v_h = v_ref[n, :, h * GLA_DV:(h + 1) * GLA_DV]
            a = jnp.dot(d["qm"][h], d["kTd"][p * 128:(p + 1) * 128, :], preferred_element_type=F32)
            a = jnp.where(tril_bd, a, 0.0).astype(BF16)
            o = jnp.dot(a, v_h, preferred_element_type=F32)
            o = o + jnp.concatenate(
                [jnp.dot(d["qm"][h][c * C:(c + 1) * C],
                         jnp.concatenate([d["states"][2 * p][c], d["states"][2 * p + 1][c]], axis=0),
                         preferred_element_type=F32) for c in range(nc)], axis=0)
            y = o * lax.rsqrt(jnp.mean(o * o, axis=-1, keepdims=True) + EPS) * gn
            r = r_ref[n, :, h * GLA_DV:(h + 1) * GLA_DV].astype(F32)
            o_ref[n, :, h * GLA_DV:(h + 1) * GLA_DV] = (y * (r * _sigmoid(r))).astype(BF16)

    work = [dict() for _ in range(NB)]
    for stage in (log_decay, decayed_operands, state_scan, outputs):
        for n in range(NB):
            stage(n, work[n])


def _gla(tok, fm, layer, waT, baT, gn):
    B, S, _ = tok.shape
    T = GLA_STEP
    NB = GLA_BATCH_ROWS
    assert B % NB == 0 and S % T == 0
    const = lambda shape: _layer_spec(shape, layer)
    return pl.pallas_call(
        _gla_kernel,
        grid=(B // NB, S // T),
        in_specs=[
            pl.BlockSpec((NB, T, 256), lambda b, s: (b, s, TK_QG // 256)),
            pl.BlockSpec((NB, 256, T), lambda b, s: (b, FM_KG // 256, s)),
            pl.BlockSpec((NB, T, 512), lambda b, s: (b, s, TK_VG // 512)),
            pl.BlockSpec((NB, 16, T), lambda b, s: (b, FM_AL // 16, s)),
            pl.BlockSpec((NB, T, 512), lambda b, s: (b, s, TK_RG // 512)),
            const((256, 16)), const((256, LANES)), const((1, GLA_DV)),
        ],
        out_specs=pl.BlockSpec((NB, T, 512), lambda b, s: (b, s, 0)),
        out_shape=jax.ShapeDtypeStruct((B, S, 512), BF16),
        scratch_shapes=[pltpu.VMEM((NB, GLA_HEADS * GLA_DK, GLA_DV), F32)],
        compiler_params=_cparams(("parallel", "arbitrary")),
        name="gla",
    )(tok, fm, tok, fm, tok, waT, baT, gn)


def _post_kernel(x_ref, q_ref, kc_ref, kp_ref, vc_ref, vp_ref, sink_ref, ybT_ref, yc_ref, g_ref,
                 ws_ref, wm_ref, wg_ref, wo_ref, fn_ref, wfg_ref, wfu_ref, wfd_ref, o_ref,
                 h_scr, ya_scr, k_scr, *, tiles_per_row, num_tiles):
    n = pl.program_id(0)
    tm = x_ref.shape[1]
    D = D_MODEL
    tq = SWA_TQ
    assert tq == SWA_WINDOW
    grp = SWA_Q_HEADS // SWA_KV_HEADS
    slot_w = lax.rem(n, 2)
    slot_r = 1 - slot_w

    @pl.when(n == 0)
    def _():
        ya_scr[...] = jnp.zeros_like(ya_scr)

    head_row = _iota_div((LANES, LANES), 0, HEAD_DIM)
    for j in range(tm // LANES + 1):
        kt = (kp_ref[0] if j == 0 else kc_ref[0, :, (j - 1) * LANES:j * LANES]).astype(F32)
        for g in range(SWA_KV_HEADS):
            k_scr[g, j * LANES:(j + 1) * LANES, :] = jnp.where(head_row == g, kt, 0.0).T.astype(BF16)
    r = lax.broadcasted_iota(jnp.int32, (2 * tq, tq), 0)
    c = lax.broadcasted_iota(jnp.int32, (2 * tq, tq), 1)
    bias_two = jnp.where((r > c) & (r <= c + tq), 0.0, NEG).astype(F32)
    tile_in_row = lax.rem(jnp.minimum(n, num_tiles - 1), tiles_per_row)
    seq_start = jnp.full((2 * tq, tq), tile_in_row, jnp.int32) == 0
    bias_first = jnp.where(seq_start & (r < tq), NEG, bias_two)

    def scores(g, i):
        qg = jnp.concatenate(
            [q_ref[0, (g * grp + hh) * HEAD_DIM:(g * grp + hh + 1) * HEAD_DIM, i * tq:(i + 1) * tq]
             for hh in range(grp)], axis=1)
        z = jnp.concatenate([qg, qg], axis=0)
        return jnp.dot(k_scr[g, i * tq:(i + 2) * tq, :], z, preferred_element_type=F32)

    def attend(g, i, s):
        bias = bias_two if i > 0 else bias_first
        ps, invs = [], []
        for hh in range(grp):
            h = g * grp + hh
            sh = s[:, hh * tq:(hh + 1) * tq] + bias
            sink = sink_ref[h:h + 1, :]
            m = jnp.maximum(_colmax(sh), sink)
            p = jnp.exp2(sh - m)
            den = _colsum(p) + jnp.exp2(sink - m)
            ps.append(p.astype(BF16))
            invs.append(1.0 / den)
        pcat = jnp.concatenate(ps, axis=1)
        rows = slice(g * HEAD_DIM, (g + 1) * HEAD_DIM)
        vblk = (jnp.concatenate([vp_ref[0, rows, :], vc_ref[0, rows, 0:tq]], axis=1) if i == 0
                else vc_ref[0, rows, (i - 1) * tq:(i + 1) * tq])
        o = jnp.dot(vblk, pcat, preferred_element_type=F32)
        for hh in range(grp):
            h = g * grp + hh
            ya_scr[slot_w, h * HEAD_DIM:(h + 1) * HEAD_DIM, i * tq:(i + 1) * tq] = (
                o[:, hh * tq:(hh + 1) * tq] * invs[hh]).astype(BF16)

    nsub = 2
    sub = tm // nsub
    tn = (((0,), (0,)), ((), ()))

    def branches(rs, d):
        d["pa"] = lax.dot_general(ya_scr[slot_r, :, rs], ws_ref[...], tn, preferred_element_type=F32)
        d["pb"] = lax.dot_general(ybT_ref[0, :, rs], wm_ref[...], tn, preferred_element_type=F32)
        d["pc"] = jnp.dot(yc_ref[0, rs, :], wg_ref[...], preferred_element_type=F32)

    def gate_merge(rs, d):
        d["merged"] = (_sigmoid(g_ref[0, rs, 0:D].astype(F32)) * d.pop("pa")
                       + _sigmoid(g_ref[0, rs, D:2 * D].astype(F32)) * d.pop("pb")
                       + _sigmoid(g_ref[0, rs, 2 * D:3 * D].astype(F32)) * d.pop("pc")).astype(BF16)

    def out_proj(rs, d):
        d["x1"] = x_ref[0, rs, :] + jnp.dot(d.pop("merged"), wo_ref[...], preferred_element_type=F32)

    def ffn_norm(rs, d):
        x1 = d["x1"]
        h = x1 * lax.rsqrt(jnp.mean(x1 * x1, axis=-1, keepdims=True) + EPS) * fn_ref[...]
        h_scr[rs, :] = h.astype(BF16)

    work = [(slice(t * sub, (t + 1) * sub), dict()) for t in range(nsub)]
    for stage in (branches, gate_merge, out_proj, ffn_norm):
        for rs, d in work:
            stage(rs, d)

    units = [(g, i) for i in range(tm // tq) for g in range(SWA_KV_HEADS)]
    lookahead = 2
    pending = [scores(*u) for u in units[:lookahead]]
    acc = jnp.concatenate([d.pop("x1") for _, d in work], axis=0)
    for ci, c0 in enumerate(range(0, D_FF, FFN_CHUNK)):
        c1 = c0 + FFN_CHUNK
        gt = jnp.dot(h_scr[...], wfg_ref[:, c0:c1], preferred_element_type=F32)
        up = jnp.dot(h_scr[...], wfu_ref[:, c0:c1], preferred_element_type=F32)
        a = (gt * _sigmoid(gt) * up).astype(BF16)
        acc = acc + jnp.dot(a, wfd_ref[c0:c1, :], preferred_element_type=F32)
        if ci < len(units):
            cur = pending.pop(0)
            if ci + lookahead < len(units):
                pending.append(scores(*units[ci + lookahead]))
            attend(*units[ci], cur)
    assert D_FF // FFN_CHUNK >= len(units)
    o_ref[0] = acc


def _post(x, fm, ybT, yc, tok, layer, sinks, ws, wm, wg, wo, fn, wfg, wfu, wfd):
    B, S, D = x.shape
    tm = POST_TM
    assert S % tm == 0 and tm % SWA_TQ == 0
    nt = S // tm
    N = B * nt

    def swa_tile(n):
        m = jnp.minimum(n, N - 1)
        return m // nt, m % nt

    def post_tile(n):
        m = jnp.maximum(n - 1, 0)
        return m // nt, m % nt

    def prev_tokens(n):
        b, i = swa_tile(n)
        return b, jnp.maximum(i * (tm // SWA_TQ) - 1, 0)

    swa_spec = lambda rows, row_blk: pl.BlockSpec((1, rows, tm), lambda n: (swa_tile(n)[0], row_blk, swa_tile(n)[1]))
    prev_spec = lambda row_blk: pl.BlockSpec((1, 128, SWA_TQ), lambda n: (prev_tokens(n)[0], row_blk, prev_tokens(n)[1]))
    tok_major = lambda cols, col_blk: pl.BlockSpec((1, tm, cols), lambda n: (post_tile(n)[0], post_tile(n)[1], col_blk))
    const = lambda shape: _layer_spec(shape, layer, single_buffer=True)
    return pl.pallas_call(
        functools.partial(_post_kernel, tiles_per_row=nt, num_tiles=N),
        grid=(N + 1,),
        in_specs=[
            tok_major(D, 0),
            swa_spec(512, FM_QA // 512),
            swa_spec(128, FM_KA // 128), prev_spec(FM_KA // 128),
            swa_spec(128, FM_VA // 128), prev_spec(FM_VA // 128),
            const((SWA_Q_HEADS, SWA_TQ)),
            pl.BlockSpec((1, 512, tm), lambda n: (post_tile(n)[0], 0, post_tile(n)[1])),
            tok_major(512, 0),
            tok_major(N_BRANCH * D, TK_GATES),
            const((512, D)), const((512, D)), const((512, D)), const((D, D)),
            const((1, D)), const((D, D_FF)), const((D, D_FF)), const((D_FF, D)),
        ],
        out_specs=tok_major(D, 0),
        out_shape=jax.ShapeDtypeStruct((B, S, D), F32),
        scratch_shapes=[pltpu.VMEM((tm, D), BF16), pltpu.VMEM((2, 512, tm), BF16),
                        pltpu.VMEM((SWA_KV_HEADS, tm + LANES, LANES), BF16)],
        compiler_params=_cparams(("arbitrary",)),
        name="post",
    )(x, fm, fm, fm, fm, fm, sinks, ybT, yc, tok, ws, wm, wg, wo, fn, wfg, wfu, wfd)


def _cols(w, *ranges):
    return jnp.concatenate([w[..., a:b] for a, b in ranges], axis=-1)


def _prep_weights(attn_norm, w_in, swa_qn, swa_kn, swa_sinks, moba_qn, moba_kn, w_alpha, b_alpha, gla_out_norm,
                  w_branch_swa, w_branch_moba, w_branch_gla, w_out, ffn_norm, w_ffn_gate, w_ffn_up, w_ffn_down):
    depth, D, _ = w_in.shape
    wfm = jnp.swapaxes(_cols(w_in, _C_QB, _C_KB, _C_VB, _C_QA, _C_KA, _C_VA, _C_KG, _C_AL), 1, 2).astype(BF16)
    wtok = _cols(w_in, _C_GATES, _C_VG, _C_RG, _C_QG).astype(BF16)
    q_fold = QK_SCALE * LOG2E
    gains = jnp.pad(jnp.stack([moba_qn * q_fold, moba_kn, swa_qn * q_fold, swa_kn], axis=-1),
                    ((0, 0), (0, 0), (0, LANES - 4)))
    return dict(
        attn_norm=attn_norm.reshape(depth, 1, D), wtok=wtok, wfm=wfm, gains=gains,
        sinks=jnp.broadcast_to((swa_sinks * LOG2E)[..., None], (depth, SWA_Q_HEADS, SWA_TQ)),
        waT=jnp.swapaxes(w_alpha, 1, 2).astype(BF16),
        baT=jnp.broadcast_to(b_alpha[..., None], b_alpha.shape + (LANES,)),
        gn=gla_out_norm[:, None, :],
        ws=w_branch_swa.astype(BF16), wm=w_branch_moba.astype(BF16), wg=w_branch_gla.astype(BF16),
        wo=w_out.astype(BF16), ffn_norm=ffn_norm.reshape(depth, 1, D),
        wfg=w_ffn_gate.astype(BF16), wfu=w_ffn_up.astype(BF16), wfd=w_ffn_down.astype(BF16))


def kernel(x, positions, attn_norm, w_in, swa_q_norm, swa_k_norm, swa_sinks, moba_q_norm, moba_k_norm,
           gla_w_alpha, gla_b_alpha, gla_out_norm, w_branch_swa, w_branch_moba, w_branch_gla, w_out,
           ffn_norm, w_ffn_gate, w_ffn_up, w_ffn_down):
    B, S, D = x.shape
    depth = w_in.shape[0]
    pos3 = positions.reshape(B, 1, S)
    inv_freq = ROPE_THETA ** (-jnp.arange(0, HEAD_DIM, 2, dtype=F32) / HEAD_DIM)
    invf = jnp.broadcast_to(inv_freq.reshape(-1, 1), (HEAD_DIM // 2, LANES))
    rope = _rope_tables(pos3, invf)
    w = _prep_weights(attn_norm, w_in, swa_q_norm, swa_k_norm, swa_sinks, moba_q_norm, moba_k_norm,
                      gla_w_alpha, gla_b_alpha, gla_out_norm, w_branch_swa, w_branch_moba, w_branch_gla,
                      w_out, ffn_norm, w_ffn_gate, w_ffn_up, w_ffn_down)
    for l in range(depth):
        tok, fm = _in_proj(x, rope, l, w["attn_norm"], w["wtok"], w["wfm"], w["gains"])
        ybT = _moba(fm)
        yc = _gla(tok, fm, l, w["waT"], w["baT"], w["gn"])
        x = _post(x, fm, ybT, yc, tok, l, w["sinks"], w["ws"], w["wm"], w["wg"], w["wo"],
                  w["ffn_norm"], w["wfg"], w["wfu"], w["wfd"])
    return x
```
